```python
import math
import jax
import jax.numpy as jnp
from jax import lax
import numpy as np

D_MODEL = 2048
BATCH = 8
SEQ = 4096
DEPTH = 4

CHUNK = 64
N_MIXERS = 2
N_RWKV = (DEPTH + N_MIXERS - 1) // N_MIXERS
N_MAMBA = DEPTH // N_MIXERS
RWKV_HEAD = 64
RWKV_HEADS = D_MODEL // RWKV_HEAD
LORA_DECAY = 96
LORA_ICLR = 96
LORA_VALUE = 64
LORA_GATE = 256
GN_EPS = 64e-5
SSM_INNER = 2 * D_MODEL
SSM_HEAD = 64
SSM_HEADS = SSM_INNER // SSM_HEAD
SSM_GROUPS = 8
SSM_HPG = SSM_HEADS // SSM_GROUPS
SSM_STATE = 128
SSM_CONV = 4
SSM_BC = SSM_GROUPS * SSM_STATE
SSM_CONV_DIM = SSM_INNER + 2 * SSM_BC
SSM_PROJ = SSM_INNER + SSM_CONV_DIM + SSM_HEADS
FFN_HIDDEN = 5504
FFN_CONV = 3
RMS_EPS = 1e-6
N_MOD = 6

kernel_name = "rwkv7_mamba2_convffn_sandwich_adaln_trunk"


def _rms(x, g):
    xf = x.astype(jnp.float32)
    y = xf * lax.rsqrt(jnp.mean(xf * xf, axis=-1, keepdims=True) + RMS_EPS)
    return (y * g).astype(x.dtype)


def _token_shift(x):
    return jnp.pad(x, ((0, 0), (1, 0), (0, 0)))[:, :-1]


def _causal_dwconv(x, w, b):
    width, ch = w.shape
    y = lax.conv_general_dilated(x, w[:, None, :], window_strides=(1,), padding=[(width - 1, 0)],
                                 dimension_numbers=("NWC", "WIO", "NWC"), feature_group_count=ch)
    return y + b


def _rwkv7_scan(r, decay, k, v, a_vec, b_vec):
    bsz, _, nh, hd = r.shape
    xs = tuple(jnp.moveaxis(t, 1, 0) for t in (r, decay, k, v, a_vec, b_vec))

    def step(state, inp):
        r_t, w_t, k_t, v_t, a_t, b_t = inp
        sa = jnp.einsum("bhvk,bhk->bhv", state, a_t)
        state = (state * w_t[:, :, None, :] + sa[..., None] * b_t[:, :, None, :]
                 + v_t[..., None] * k_t[:, :, None, :])
        return state, jnp.einsum("bhvk,bhk->bhv", state, r_t)

    s0 = jnp.zeros((bsz, nh, hd, hd), jnp.float32)
    _, y = lax.scan(step, s0, xs)
    return jnp.moveaxis(y, 0, 1)


def _rwkv7_mix(h, v_first, mu, w_rkv, w0, w1, w2, a0, a1, a2, v_lora, g1, g2,
               k_k, k_a, r_k, ln_w, ln_b, w_o):
    bsz, seq, d = h.shape
    xx = _token_shift(h) - h
    x_rkv = h[None] + xx[None] * mu[:3, None, None, :]
    r, k, v = jnp.einsum("nbtd,nde->nbte", x_rkv, w_rkv)
    xw = h + xx * mu[3]
    xa = h + xx * mu[4]
    xg = h + xx * mu[5]
    w = -jax.nn.softplus(-(w0 + jnp.tanh(xw @ w1) @ w2)) - 0.5
    decay = jnp.exp(-jnp.exp(w.astype(jnp.float32)))
    a = jax.nn.sigmoid(a0 + (xa @ a1) @ a2)
    g = jax.nn.sigmoid(xg @ g1) @ g2
    if v_first is None:
        v_first = v
    else:
        v0, v1, v2 = v_lora
        v = v + (v_first - v) * jax.nn.sigmoid(v0 + (x_rkv[2] @ v1) @ v2)
    heads = lambda t: t.astype(jnp.float32).reshape(bsz, seq, RWKV_HEADS, RWKV_HEAD)
    kk = heads(k * k_k)
    kk = kk * lax.rsqrt(jnp.maximum(jnp.sum(kk * kk, axis=-1, keepdims=True), 1e-24))
    k = k * (1 + (a - 1) * k_a)
    rh, kh, vh, ah = heads(r), heads(k), heads(v), heads(a)
    y = _rwkv7_scan(rh, heads(decay), kh, vh, -kk, kk * ah)
    mean = jnp.mean(y, axis=-1, keepdims=True)
    var = jnp.mean(jnp.square(y - mean), axis=-1, keepdims=True)
    y = (y - mean) * lax.rsqrt(var + GN_EPS)
    y = y * ln_w.reshape(RWKV_HEADS, RWKV_HEAD) + ln_b.reshape(RWKV_HEADS, RWKV_HEAD)
    y = y + jnp.sum(rh * kh * r_k, axis=-1, keepdims=True) * vh
    y = y.reshape(bsz, seq, d).astype(h.dtype) * g
    return y @ w_o, v_first


def _ssd_scan(xdt, adt, bm, cm):
    bsz, seq = xdt.shape[:2]
    nc = seq // CHUNK

    def chunks(t):
        return jnp.moveaxis(t.reshape(bsz, nc, CHUNK, *t.shape[2:]), 1, 0)

    mask = jnp.tril(jnp.ones((CHUNK, CHUNK), bool))[None, :, :, None, None]

    def step(state, inp):
        x_c, a_c, b_c, c_c = inp
        acs = jnp.cumsum(a_c, axis=1)
        seg = acs[:, :, None] - acs[:, None, :]
        lmat = jnp.exp(jnp.where(mask, seg, -jnp.inf))
        cb = jnp.einsum("blgn,bsgn->blsg", c_c, b_c)
        y = jnp.einsum("blsg,blsgr,bsgrp->blgrp", cb, lmat, x_c)
        y = y + jnp.einsum("blgn,bgrpn->blgrp", c_c, state) * jnp.exp(acs)[..., None]
        w_end = jnp.exp(acs[:, -1:] - acs)
        state = (state * jnp.exp(acs[:, -1])[..., None, None]
                 + jnp.einsum("blgn,blgr,blgrp->bgrpn", b_c, w_end, x_c))
        return state, y

    s0 = jnp.zeros((bsz, SSM_GROUPS, SSM_HPG, SSM_HEAD, SSM_STATE), jnp.float32)
    _, y = lax.scan(step, s0, (chunks(xdt), chunks(adt), chunks(bm), chunks(cm)))
    return jnp.moveaxis(y, 0, 1).reshape(bsz, seq, SSM_GROUPS, SSM_HPG, SSM_HEAD)


def _mamba2_mix(h, w_in, conv_w, conv_b, dt_bias, a_log, d_skip, norm_w, w_out):
    bsz, seq, _ = h.shape
    zxbcdt = h @ w_in
    z = zxbcdt[..., :SSM_INNER]
    xbc = zxbcdt[..., SSM_INNER:SSM_INNER + SSM_CONV_DIM]
    dt = zxbcdt[..., SSM_INNER + SSM_CONV_DIM:]
    xbc = jax.nn.silu(_causal_dwconv(xbc, conv_w, conv_b)).astype(jnp.float32)
    xs = xbc[..., :SSM_INNER].reshape(bsz, seq, SSM_GROUPS, SSM_HPG, SSM_HEAD)
    bm = xbc[..., SSM_INNER:SSM_INNER + SSM_BC].reshape(bsz, seq, SSM_GROUPS, SSM_STATE)
    cm = xbc[..., SSM_INNER + SSM_BC:].reshape(bsz, seq, SSM_GROUPS, SSM_STATE)
    dt = jax.nn.softplus(dt.astype(jnp.float32) + dt_bias).reshape(bsz, seq, SSM_GROUPS, SSM_HPG)
    a = -jnp.exp(a_log.astype(jnp.float32)).reshape(SSM_GROUPS, SSM_HPG)
    y = _ssd_scan(xs * dt[..., None], a * dt, bm, cm)
    y = y + xs * d_skip.reshape(SSM_GROUPS, SSM_HPG, 1)
    y = y.reshape(bsz, seq, SSM_INNER) * jax.nn.silu(z.astype(jnp.float32))
    y = _rms(y, norm_w).astype(h.dtype)
    return y @ w_out


def _conv_ffn(h, w_in, conv_w, conv_b, w_out):
    u = _causal_dwconv(h @ w_in, conv_w, conv_b)
    gate, val = jnp.split(u, 2, axis=-1)
    return (jax.nn.silu(gate) * val) @ w_out


def setup_inputs(seed: int = 0) -> dict:
    key = jax.random.key(seed)
    ks = iter(jax.random.split(key, 64))
    nrm = lambda shape, scale: scale * jax.random.normal(next(ks), shape, jnp.float32)
    uni = lambda shape, lo, hi: jax.random.uniform(next(ks), shape, jnp.float32, lo, hi)
    d = D_MODEL
    sd = d ** -0.5
    nv = max(N_RWKV - 1, 0)
    dt0 = jnp.exp(uni((N_MAMBA, SSM_HEADS), math.log(1e-3), math.log(1e-1)))
    return {
        "x": nrm((BATCH, SEQ, d), 1.0),
        "c": nrm((BATCH, d), 1.0),
        "ada_w": nrm((d, N_MOD * d), 0.5 * sd),
        "ada_b": nrm((N_MOD * d,), 0.02),
        "ada_table": nrm((DEPTH, N_MOD * d), 0.1),
        "norm_mix_pre": 1.0 + nrm((DEPTH, d), 0.1),
        "norm_mix_post": 1.0 + nrm((DEPTH, d), 0.1),
        "norm_ffn_pre": 1.0 + nrm((DEPTH, d), 0.1),
        "norm_ffn_post": 1.0 + nrm((DEPTH, d), 0.1),
        "rwkv_mu": uni((N_RWKV, 6, d), 0.0, 1.0),
        "rwkv_w_rkv": nrm((N_RWKV, 3, d, d), sd),
        "rwkv_w0": uni((N_RWKV, d), -6.0, -1.0),
        "rwkv_w1": nrm((N_RWKV, d, LORA_DECAY), sd),
        "rwkv_w2": nrm((N_RWKV, LORA_DECAY, d), 0.5 * LORA_DECAY ** -0.5),
        "rwkv_a0": nrm((N_RWKV, d), 0.1),
        "rwkv_a1": nrm((N_RWKV, d, LORA_ICLR), sd),
        "rwkv_a2": nrm((N_RWKV, LORA_ICLR, d), 0.5 * LORA_ICLR ** -0.5),
        "rwkv_v0": 1.0 + nrm((nv, d), 0.1),
        "rwkv_v1": nrm((nv, d, LORA_VALUE), sd),
        "rwkv_v2": nrm((nv, LORA_VALUE, d), 0.5 * LORA_VALUE ** -0.5),
        "rwkv_g1": nrm((N_RWKV, d, LORA_GATE), sd),
        "rwkv_g2": nrm((N_RWKV, LORA_GATE, d), LORA_GATE ** -0.5),
        "rwkv_k_k": 0.85 + nrm((N_RWKV, d), 0.05),
        "rwkv_k_a": 1.0 + nrm((N_RWKV, d), 0.05),
        "rwkv_r_k": nrm((N_RWKV, RWKV_HEADS, RWKV_HEAD), 0.1),
        "rwkv_ln_w": 1.0 + nrm((N_RWKV, d), 0.1),
        "rwkv_ln_b": nrm((N_RWKV, d), 0.01),
        "rwkv_w_o": nrm((N_RWKV, d, d), sd),
        "ssm_w_in": nrm((N_MAMBA, d, SSM_PROJ), sd),
        "ssm_conv_w": nrm((N_MAMBA, SSM_CONV, SSM_CONV_DIM), SSM_CONV ** -0.5),
        "ssm_conv_b": nrm((N_MAMBA, SSM_CONV_DIM), 0.01),
        "ssm_dt_bias": dt0 + jnp.log(-jnp.expm1(-dt0)),
        "ssm_a_log": jnp.log(uni((N_MAMBA, SSM_HEADS), 1.0, 16.0)),
        "ssm_d": 1.0 + nrm((N_MAMBA, SSM_HEADS), 0.1),
        "ssm_norm": 1.0 + nrm((N_MAMBA, SSM_INNER), 0.1),
        "ssm_w_out": nrm((N_MAMBA, SSM_INNER, d), SSM_INNER ** -0.5),
        "ffn_w_in": nrm((DEPTH, d, 2 * FFN_HIDDEN), sd),
        "ffn_conv_w": nrm((DEPTH, FFN_CONV, 2 * FFN_HIDDEN), FFN_CONV ** -0.5),
        "ffn_conv_b": nrm((DEPTH, 2 * FFN_HIDDEN), 0.01),
        "ffn_w_out": nrm((DEPTH, FFN_HIDDEN, d), FFN_HIDDEN ** -0.5),
    }


def reference(x, c, ada_w, ada_b, ada_table, norm_mix_pre, norm_mix_post, norm_ffn_pre,
              norm_ffn_post, rwkv_mu, rwkv_w_rkv, rwkv_w0, rwkv_w1, rwkv_w2, rwkv_a0, rwkv_a1,
              rwkv_a2, rwkv_v0, rwkv_v1, rwkv_v2, rwkv_g1, rwkv_g2, rwkv_k_k, rwkv_k_a, rwkv_r_k,
              rwkv_ln_w, rwkv_ln_b, rwkv_w_o, ssm_w_in, ssm_conv_w, ssm_conv_b, ssm_dt_bias,
              ssm_a_log, ssm_d, ssm_norm, ssm_w_out, ffn_w_in, ffn_conv_w, ffn_conv_b, ffn_w_out):
    mod = jax.nn.silu(c) @ ada_w + ada_b
    v_first = None
    for layer in range(DEPTH):
        sh_m, sc_m, g_m, sh_f, sc_f, g_f = jnp.split((mod + ada_table[layer])[:, None, :], N_MOD, axis=-1)
        h = _rms(x, norm_mix_pre[layer]) * (1 + sc_m) + sh_m
        idx = layer // N_MIXERS
        if layer % N_MIXERS == 0:
            v_lora = None if idx == 0 else (rwkv_v0[idx - 1], rwkv_v1[idx - 1], rwkv_v2[idx - 1])
            y, v_first = _rwkv7_mix(h, v_first, rwkv_mu[idx], rwkv_w_rkv[idx], rwkv_w0[idx],
                                    rwkv_w1[idx], rwkv_w2[idx], rwkv_a0[idx], rwkv_a1[idx],
                                    rwkv_a2[idx], v_lora, rwkv_g1[idx], rwkv_g2[idx],
                                    rwkv_k_k[idx], rwkv_k_a[idx], rwkv_r_k[idx],
                                    rwkv_ln_w[idx], rwkv_ln_b[idx], rwkv_w_o[idx])
        else:
            y = _mamba2_mix(h, ssm_w_in[idx], ssm_conv_w[idx], ssm_conv_b[idx], ssm_dt_bias[idx],
                            ssm_a_log[idx], ssm_d[idx], ssm_norm[idx], ssm_w_out[idx])
        x = x + g_m * _rms(y, norm_mix_post[layer])
        h = _rms(x, norm_ffn_pre[layer]) * (1 + sc_f) + sh_f
        f = _conv_ffn(h, ffn_w_in[layer], ffn_conv_w[layer], ffn_conv_b[layer], ffn_w_out[layer])
        x = x + g_f * _rms(f, norm_ffn_post[layer])
    return x
```

```python
import functools

import jax
import jax.numpy as jnp
from jax import lax
from jax.experimental import pallas as pl
from jax.experimental.pallas import tpu as pltpu

F32 = jnp.float32
BF16 = jnp.bfloat16
HIGHEST = lax.Precision.HIGHEST

D_MODEL = 2048
N_MOD = 6
RMS_EPS = 1e-6
GN_EPS = 64e-5
HEAD = 64
CHUNK = 64
RWKV_LANES = 256
RWKV_HB = RWKV_LANES // HEAD
SSM_INNER = 2 * D_MODEL
SSM_GROUPS = 8
SSM_HPG = 8
SSM_STATE = 128
SSM_GL = SSM_HPG * HEAD
SSM_BC = SSM_GROUPS * SSM_STATE
SSM_CONV_DIM = SSM_INNER + 2 * SSM_BC
SSM_HEADS = SSM_GROUPS * SSM_HPG
FFN_HIDDEN = 5504
FFN_PAD = 5632
HALO = 16
VMEM_LIMIT = 56 * 1024 * 1024


def _cparams(sem):
    return pltpu.CompilerParams(dimension_semantics=sem, vmem_limit_bytes=VMEM_LIMIT)


def _bdot(a, b):
    return jnp.dot(a.astype(BF16), b.astype(BF16), preferred_element_type=F32)


def _bdot_nt(a, b):
    return lax.dot_general(a.astype(BF16), b.astype(BF16), (((1,), (1,)), ((), ())),
                           preferred_element_type=F32)


def _bdot_tn(a, b):
    return lax.dot_general(a.astype(BF16), b.astype(BF16), (((0,), (0,)), ((), ())),
                           preferred_element_type=F32)


def _hdot(a, b):
    return jnp.dot(a, b, preferred_element_type=F32, precision=HIGHEST)


def _softplus(z):
    return jnp.maximum(z, 0.0) + jnp.log1p(jnp.exp(-jnp.abs(z)))


def _silu(z):
    return z * jax.nn.sigmoid(z)


def _prenorm_kernel(x_ref, xh_ref, g_ref, sc_ref, sh_ref, *rest, n_mix, blocks_per_seq):
    def norm(xv):
        ms = jnp.mean(xv * xv, axis=-1, keepdims=True)
        return xv * lax.rsqrt(ms + RMS_EPS) * g_ref[...] * (1.0 + sc_ref[...]) + sh_ref[...]

    h = norm(x_ref[...])
    if n_mix == 0:
        rest[0][...] = h.astype(rest[0].dtype)
        return
    mu_ref, o_refs = rest[0], rest[1:]
    prev = norm(xh_ref[...])[HALO - 1:HALO, :]
    first = (pl.program_id(0) % blocks_per_seq) == 0
    prev = jnp.where(first, 0.0, prev)
    row = lax.broadcasted_iota(jnp.int32, h.shape, 0)
    hprev = jnp.where(row == 0, prev, pltpu.roll(h, 1, axis=0))
    xx = hprev - h
    for m in range(n_mix):
        o_refs[m][...] = (h + xx * mu_ref[m:m + 1, :]).astype(o_refs[m].dtype)


def _prenorm(x, g, sc, sh, mu, seq, tm=256):
    n, d = x.shape
    nb = seq // tm
    n_mix = 0 if mu is None else mu.shape[0]
    row = lambda i: (i, 0)
    halo = lambda i: (jnp.maximum(i * (tm // HALO) - 1, 0), 0)
    per_b = lambda i: (i // nb, 0, 0)
    in_specs = [pl.BlockSpec((tm, d), row), pl.BlockSpec((HALO, d), halo),
                pl.BlockSpec((1, d), lambda i: (0, 0)),
                pl.BlockSpec((None, 1, d), per_b), pl.BlockSpec((None, 1, d), per_b)]
    args = [x, x, g.reshape(1, d), sc[:, None, :], sh[:, None, :]]
    if n_mix:
        in_specs.append(pl.BlockSpec((n_mix, d), lambda i: (0, 0)))
        args.append(mu)
    n_out = max(n_mix, 1)
    out = pl.pallas_call(
        functools.partial(_prenorm_kernel, n_mix=n_mix, blocks_per_seq=nb),
        grid=(n // tm,),
        in_specs=in_specs,
        out_specs=[pl.BlockSpec((tm, d), row)] * n_out,
        out_shape=[jax.ShapeDtypeStruct((n, d), BF16)] * n_out,
        compiler_params=_cparams(("parallel",)),
        name="prenorm",
    )(*args)
    return out


def _mm_kernel(x_ref, w_ref, *rest, act, has_bias):
    o_ref = rest[-1]
    acc = jnp.dot(x_ref[...], w_ref[...], preferred_element_type=F32)
    if has_bias:
        acc = acc + rest[0][...]
    if act == "tanh":
        acc = jnp.tanh(acc)
    elif act == "sigmoid":
        acc = jax.nn.sigmoid(acc)
    o_ref[...] = acc.astype(o_ref.dtype)


def _mm(x, w, *, act=None, bias=None, out_dtype=F32, tm=1024, tn=1024):
    m, k = x.shape
    n = w.shape[1]
    tm = min(tm, m)
    tn = min(tn, n)
    in_specs = [pl.BlockSpec((tm, k), lambda i, j: (i, 0)), pl.BlockSpec((k, tn), lambda i, j: (0, j))]
    args = [x, w]
    if bias is not None:
        in_specs.append(pl.BlockSpec((1, tn), lambda i, j: (0, j)))
        args.append(bias.reshape(1, n))
    return pl.pallas_call(
        functools.partial(_mm_kernel, act=act, has_bias=bias is not None),
        grid=(m // tm, n // tn),
        in_specs=in_specs,
        out_specs=pl.BlockSpec((tm, tn), lambda i, j: (i, j)),
        out_shape=jax.ShapeDtypeStruct((m, n), out_dtype),
        compiler_params=_cparams(("parallel", "parallel")),
        name="mm",
    )(*args)


def _mm_conv_kernel(x_ref, xh_ref, *rest, taps, n_branch, blocks_per_seq, tm):
    w_refs = rest[:n_branch]
    cw_refs = rest[n_branch:2 * n_branch]
    cb_refs = rest[2 * n_branch:3 * n_branch]
    o_ref = rest[3 * n_branch]
    u_ref = rest[3 * n_branch + 1]
    first = (pl.program_id(0) % blocks_per_seq) == 0
    xh = xh_ref[...]
    xh = jnp.where(first, jnp.zeros_like(xh), xh)
    x = x_ref[...]

    def branch(b):
        w = w_refs[b][...]
        u_ref[0:HALO, :] = jnp.dot(xh, w, preferred_element_type=F32)
        u_ref[HALO:, :] = jnp.dot(x, w, preferred_element_type=F32)
        acc = cb_refs[b][...]
        for j in range(taps):
            acc = acc + cw_refs[b][j:j + 1, :] * u_ref[pl.ds(HALO - (taps - 1) + j, tm), :]
        return acc

    y = _silu(branch(0))
    if n_branch == 2:
        y = y * branch(1)
    o_ref[...] = y.astype(o_ref.dtype)


def _mm_conv(x, ws, cws, cbs, seq, *, out_dtype, tm=1024, tn=512):
    m, k = x.shape
    n = ws[0].shape[1]
    taps = cws[0].shape[0]
    tm = min(tm, seq)
    nb = seq // tm
    nbr = len(ws)
    col = lambda i, j: (0, j)
    in_specs = ([pl.BlockSpec((tm, k), lambda i, j: (i, 0)),
                 pl.BlockSpec((HALO, k), lambda i, j: (jnp.maximum(i * (tm // HALO) - 1, 0), 0))]
                + [pl.BlockSpec((k, tn), col)] * nbr
                + [pl.BlockSpec((taps, tn), col)] * nbr
                + [pl.BlockSpec((1, tn), col)] * nbr)
    return pl.pallas_call(
        functools.partial(_mm_conv_kernel, taps=taps, n_branch=nbr, blocks_per_seq=nb, tm=tm),
        grid=(m // tm, n // tn),
        in_specs=in_specs,
        out_specs=pl.BlockSpec((tm, tn), lambda i, j: (i, j)),
        out_shape=jax.ShapeDtypeStruct((m, n), out_dtype),
        scratch_shapes=[pltpu.VMEM((HALO + tm, tn), F32)],
        compiler_params=_cparams(("parallel", "parallel")),
        name="mm_conv",
    )(x, x, *ws, *cws, *[b.reshape(1, n) for b in cbs])


def _mm_post_kernel(a_ref, w_ref, res_ref, gate_ref, g_ref, *rest, nk, ssq_dim):
    if ssq_dim:
        ssq_ref, o_ref, acc_ref = rest
    else:
        o_ref, acc_ref = rest
    kk = pl.program_id(1)

    @pl.when(kk == 0)
    def _():
        acc_ref[...] = jnp.zeros_like(acc_ref)

    acc_ref[...] += jnp.dot(a_ref[...], w_ref[...], preferred_element_type=F32)

    @pl.when(kk == nk - 1)
    def _():
        f = acc_ref[...]
        if ssq_dim:
            f = f * lax.rsqrt(ssq_ref[:, 0:1] * (1.0 / ssq_dim) + RMS_EPS)
        ms = jnp.mean(f * f, axis=-1, keepdims=True)
        y = f * lax.rsqrt(ms + RMS_EPS) * g_ref[...]
        o_ref[...] = res_ref[...] + gate_ref[...] * y


def _mm_post(a, w, res, gate, g, seq, *, ssq=None, ssq_dim=0, tm=512, tk=512):
    m, k = a.shape
    d = w.shape[1]
    tm = min(tm, seq)
    nk = k // tk
    nb = seq // tm
    row = lambda i, kk: (i, 0)
    in_specs = [pl.BlockSpec((tm, tk), lambda i, kk: (i, kk)),
                pl.BlockSpec((tk, d), lambda i, kk: (kk, 0)),
                pl.BlockSpec((tm, d), row),
                pl.BlockSpec((None, 1, d), lambda i, kk: (i // nb, 0, 0)),
                pl.BlockSpec((1, d), lambda i, kk: (0, 0))]
    args = [a, w, res, gate[:, None, :], g.reshape(1, d)]
    if ssq is not None:
        in_specs.append(pl.BlockSpec((tm, 128), row))
        args.append(ssq)
    return pl.pallas_call(
        functools.partial(_mm_post_kernel, nk=nk, ssq_dim=ssq_dim),
        grid=(m // tm, nk),
        in_specs=in_specs,
        out_specs=pl.BlockSpec((tm, d), row),
        out_shape=jax.ShapeDtypeStruct((m, d), F32),
        scratch_shapes=[pltpu.VMEM((tm, d), F32)],
        compiler_params=_cparams(("parallel", "arbitrary")),
        name="mm_post",
    )(*args)


def _rwkv_scan_kernel(*refs, tc, has_vmix):
    it = iter(refs)
    r_ref, k_ref, v_ref, wl_ref, al_ref, g_ref = (next(it) for _ in range(6))
    if has_vmix:
        vf_ref, vl_ref, v0_ref = (next(it) for _ in range(3))
    w0_ref, a0_ref, kk_ref, ka_ref, rk_ref, lnw_ref, lnb_ref = (next(it) for _ in range(7))
    o_ref = next(it)
    s_ref, rs, ks, vs, lws, as_, bs, ys = (next(it) for _ in range(8))

    L = RWKV_LANES
    C = CHUNK

    @pl.when(pl.program_id(2) == 0)
    def _():
        s_ref[...] = jnp.zeros_like(s_ref)

    ri = lax.broadcasted_iota(jnp.int32, (L, L), 0)
    ci = lax.broadcasted_iota(jnp.int32, (L, L), 1)
    bd = (ri // HEAD) == (ci // HEAD)
    bd_ones = jnp.where(bd, 1.0, 0.0).astype(F32)
    t_i = lax.broadcasted_iota(jnp.int32, (C, L), 0)
    s_i = lax.broadcasted_iota(jnp.int32, (C, L), 1) % HEAD
    strict = s_i < t_i
    incl = s_i <= t_i
    eye_cat = jnp.where(s_i == t_i, 1.0, 0.0).astype(F32)
    tri = jnp.where(lax.broadcasted_iota(jnp.int32, (C, C), 1) <= lax.broadcasted_iota(jnp.int32, (C, C), 0),
                    1.0, 0.0).astype(F32)

    def headsum(xv):
        return _hdot(xv, bd_ones)

    r = r_ref[...]
    k0 = k_ref[...]
    v = v_ref[...]
    if has_vmix:
        v = v + (vf_ref[...] - v) * jax.nn.sigmoid(v0_ref[...] + vl_ref[...])
    w = -_softplus(-(w0_ref[...] + wl_ref[...])) - 0.5
    a = jax.nn.sigmoid(a0_ref[...] + al_ref[...])
    kk = k0 * kk_ref[...]
    kk = kk * lax.rsqrt(jnp.maximum(headsum(kk * kk), 1e-24))
    k = k0 * (1.0 + (a - 1.0) * ka_ref[...])
    rs[...] = r
    ks[...] = k
    vs[...] = v
    lws[...] = -jnp.exp(w)
    as_[...] = -kk
    bs[...] = kk * a

    def vstack(xv):
        return jnp.where(bd, jnp.concatenate([xv] * RWKV_HB, axis=0), 0.0)

    def chunk(c, carry):
        rows = pl.ds(pl.multiple_of(c * C, C), C)
        lw = lws[rows, :]
        rr, kc, vc, ac, bc = rs[rows, :], ks[rows, :], vs[rows, :], as_[rows, :], bs[rows, :]
        cum = _hdot(tri, lw)
        cum_end = cum[C - 1:C, :]
        e_pos = jnp.exp(cum)
        e_neg = jnp.exp(-cum)
        e_end = jnp.exp(cum_end - cum)
        At = ac * jnp.exp(cum - lw)
        Rt = rr * e_pos
        Kt = kc * e_neg
        Bt = bc * e_neg
        Kh = kc * e_end
        Bh = bc * e_end
        AR = jnp.concatenate([At, Rt], axis=0)
        W2 = jnp.concatenate([vstack(Bt), vstack(Kt)], axis=0)
        G = _bdot_nt(AR, W2)
        A_ab = jnp.where(strict, G[:C, :L], 0.0)
        A_ak = jnp.where(strict, G[:C, L:], 0.0)
        A_rb = jnp.where(incl, G[C:, :L], 0.0)
        A_rk = jnp.where(incl, G[C:, L:], 0.0)
        T = eye_cat + A_ab
        P = _bdot(A_ab, vstack(A_ab))
        for _ in range(4):
            res = _bdot(P, jnp.concatenate([vstack(T), vstack(P)], axis=1))
            T = T + res[:, :L]
            P = res[:, L:]
        T = T + _bdot(P, vstack(T))
        S = s_ref[...]
        AS = _bdot(AR, S)
        AV = _bdot(jnp.concatenate([A_ak, A_rk], axis=0), vstack(vc))
        U = _bdot(T, vstack(AS[:C] + AV[:C]))
        ys[rows, :] = AS[C:] + AV[C:] + _bdot(A_rb, vstack(U))
        g_col = jnp.transpose(jnp.broadcast_to(jnp.exp(cum_end), (L, L)))
        upd = _bdot_tn(jnp.concatenate([Kh, Bh], axis=0), jnp.concatenate([vc, U], axis=0))
        s_ref[...] = g_col * S + jnp.where(bd, upd, 0.0)
        return carry

    lax.fori_loop(0, tc // C, chunk, 0)

    y = ys[...]
    mean = headsum(y) * (1.0 / HEAD)
    yc = y - mean
    var = headsum(yc * yc) * (1.0 / HEAD)
    yn = yc * lax.rsqrt(var + GN_EPS) * lnw_ref[...] + lnb_ref[...]
    r = rs[...]
    yn = yn + headsum(r * ks[...] * rk_ref[...]) * vs[...]
    o_ref[...] = (yn * g_ref[...]).astype(o_ref.dtype)


def _rwkv_scan(r, k, v, wl, al, g, vmix, params, bsz, seq, tc=256):
    d = D_MODEL
    L = RWKV_LANES
    to3 = lambda t: t.reshape(bsz, seq, d)
    blk = pl.BlockSpec((None, tc, L), lambda b, h, t: (b, t, h))
    pblk = pl.BlockSpec((1, L), lambda b, h, t: (0, h))
    args = [to3(t) for t in (r, k, v, wl, al, g)]
    in_specs = [blk] * 6
    if vmix is not None:
        args += [to3(vmix[0]), to3(vmix[1]), vmix[2].reshape(1, d)]
        in_specs += [blk, blk, pblk]
    args += [p.reshape(1, d) for p in params]
    in_specs += [pblk] * 7
    out = pl.pallas_call(
        functools.partial(_rwkv_scan_kernel, tc=tc, has_vmix=vmix is not None),
        grid=(bsz, d // L, seq // tc),
        in_specs=in_specs,
        out_specs=blk,
        out_shape=jax.ShapeDtypeStruct((bsz, seq, d), BF16),
        scratch_shapes=[pltpu.VMEM((L, L), F32)] + [pltpu.VMEM((tc, L), F32)] * 7,
        compiler_params=_cparams(("parallel", "parallel", "arbitrary")),
        name="rwkv_scan",
    )(*args)
    return out.reshape(bsz * seq, d)


def _ssd_kernel(x_ref, b_ref, c_ref, dt_ref, z_ref, dtb_ref, alog_ref, dsk_ref, nw_ref,
                o_ref, ssq_ref, s_ref, ys, *, tc):
    C = CHUNK
    GL = SSM_GL
    g = pl.program_id(2)

    @pl.when((pl.program_id(1) == 0))
    def _():
        s_ref[g] = jnp.zeros((SSM_STATE, GL), F32)

    ej = lax.broadcasted_iota(jnp.int32, (128, GL), 0)
    ec = lax.broadcasted_iota(jnp.int32, (128, GL), 1) // HEAD
    expand = jnp.where(ej == g * SSM_HPG + ec, 1.0, 0.0).astype(F32)
    tri = jnp.where(lax.broadcasted_iota(jnp.int32, (C, C), 1) <= lax.broadcasted_iota(jnp.int32, (C, C), 0),
                    1.0, 0.0).astype(F32)
    ones_cc = jnp.ones((C, C), F32)
    t_i = lax.broadcasted_iota(jnp.int32, (C, GL), 0)
    s_i = lax.broadcasted_iota(jnp.int32, (C, GL), 1) % HEAD
    incl = s_i <= t_i
    eye_cat = jnp.where(s_i == t_i, 1.0, 0.0).astype(F32)
    ri = lax.broadcasted_iota(jnp.int32, (GL, GL), 0) // HEAD
    ci = lax.broadcasted_iota(jnp.int32, (GL, GL), 1) // HEAD
    bd = ri == ci

    dt = _softplus(dt_ref[...] + dtb_ref[...])
    a_neg = -jnp.exp(alog_ref[...])
    dt_x = _hdot(dt, expand)
    adt_x = _hdot(dt * a_neg, expand)
    dsk_x = _hdot(jnp.broadcast_to(dsk_ref[...], (8, 128)), expand)[0:1, :]

    for c in range(tc // C):
        lo, hi = c * C, (c + 1) * C
        xc = x_ref[lo:hi, :].astype(F32)
        bc = b_ref[lo:hi, :]
        cc = c_ref[lo:hi, :]
        xdt = xc * dt_x[lo:hi]
        acs = _hdot(tri, adt_x[lo:hi])
        acs_end = acs[C - 1:C, :]
        zrow = _hdot(ones_cc, acs * eye_cat)
        lmat = jnp.where(incl, jnp.exp(jnp.minimum(acs - zrow, 0.0)), 0.0)
        cb = _bdot_nt(cc, jnp.concatenate([bc] * SSM_HPG, axis=0))
        xbd = jnp.where(bd, jnp.concatenate([xdt] * SSM_HPG, axis=0), 0.0)
        S = s_ref[g]
        y = _bdot(cb * lmat, xbd) + _bdot(cc, S) * jnp.exp(acs)
        s_ref[g] = S * jnp.exp(acs_end) + _bdot_tn(bc, xdt * jnp.exp(acs_end - acs))
        ys[lo:hi, :] = y + xc * dsk_x

    yz = ys[...] * _silu(z_ref[...])
    part = jnp.sum(yz * yz, axis=-1, keepdims=True)

    @pl.when(g == 0)
    def _():
        ssq_ref[...] = jnp.zeros_like(ssq_ref)

    ssq_ref[...] += jnp.broadcast_to(part, ssq_ref.shape)
    o_ref[...] = (yz * nw_ref[...]).astype(o_ref.dtype)


def _ssd_scan(xbc, dt, z, dt_bias, a_log, d_skip, norm_w, bsz, seq, tc=256):
    G = SSM_GROUPS
    GL = SSM_GL
    xbc3 = xbc.reshape(bsz, seq, SSM_CONV_DIM)
    dt3 = dt.reshape(bsz, seq, 128)
    z3 = z.reshape(bsz, seq, SSM_INNER)
    pad128 = lambda t: jnp.pad(t.reshape(1, SSM_HEADS), ((0, 0), (0, 128 - SSM_HEADS)))
    x_spec = pl.BlockSpec((None, tc, GL), lambda b, t, g: (b, t, g))
    b_spec = pl.BlockSpec((None, tc, SSM_STATE), lambda b, t, g: (b, t, SSM_INNER // SSM_STATE + g))
    c_spec = pl.BlockSpec((None, tc, SSM_STATE), lambda b, t, g: (b, t, (SSM_INNER + SSM_BC) // SSM_STATE + g))
    dt_spec = pl.BlockSpec((None, tc, 128), lambda b, t, g: (b, t, 0))
    p128 = pl.BlockSpec((1, 128), lambda b, t, g: (0, 0))
    out, ssq = pl.pallas_call(
        functools.partial(_ssd_kernel, tc=tc),
        grid=(bsz, seq // tc, G),
        in_specs=[x_spec, b_spec, c_spec, dt_spec, x_spec, p128, p128, p128,
                  pl.BlockSpec((1, GL), lambda b, t, g: (0, g))],
        out_specs=[x_spec, dt_spec],
        out_shape=[jax.ShapeDtypeStruct((bsz, seq, SSM_INNER), BF16),
                   jax.ShapeDtypeStruct((bsz, seq, 128), F32)],
        scratch_shapes=[pltpu.VMEM((G, SSM_STATE, GL), F32), pltpu.VMEM((tc, GL), F32)],
        compiler_params=_cparams(("parallel", "arbitrary", "arbitrary")),
        name="ssd_scan",
    )(xbc3, xbc3, xbc3, dt3, z3, pad128(dt_bias), pad128(a_log), pad128(d_skip),
      norm_w.reshape(1, SSM_INNER))
    return out.reshape(bsz * seq, SSM_INNER), ssq.reshape(bsz * seq, 128)


def _pad_cols(w, n):
    return jnp.pad(w, ((0, 0), (0, n - w.shape[1])))


def _pad_rows(w, n):
    return jnp.pad(w, ((0, n - w.shape[0]), (0, 0)))


def kernel(x, c, ada_w, ada_b, ada_table, norm_mix_pre, norm_mix_post, norm_ffn_pre, norm_ffn_post, rwkv_mu, rwkv_w_rkv, rwkv_w0, rwkv_w1, rwkv_w2, rwkv_a0, rwkv_a1, rwkv_a2, rwkv_v0, rwkv_v1, rwkv_v2, rwkv_g1, rwkv_g2, rwkv_k_k, rwkv_k_a, rwkv_r_k, rwkv_ln_w, rwkv_ln_b, rwkv_w_o, ssm_w_in, ssm_conv_w, ssm_conv_b, ssm_dt_bias, ssm_a_log, ssm_d, ssm_norm, ssm_w_out, ffn_w_in, ffn_conv_w, ffn_conv_b, ffn_w_out):
    bsz, seq, d = x.shape
    depth = ada_table.shape[0]
    n = bsz * seq
    bf = lambda t: t.astype(BF16)

    sc_in = jnp.pad(bf(jax.nn.silu(c)), ((0, 16 - bsz), (0, 0)))
    mod = _mm(sc_in, bf(ada_w), bias=ada_b)[:bsz]

    xf = x.reshape(n, d)
    v_first = None
    for layer in range(depth):
        sh_m, sc_m, g_m, sh_f, sc_f, g_f = jnp.split(mod + ada_table[layer], N_MOD, axis=-1)
        idx = layer // 2
        if layer % 2 == 0:
            mu = rwkv_mu[idx]
            xr, xk, xv, xw, xa, xg = _prenorm(xf, norm_mix_pre[layer], sc_m, sh_m, mu, seq)
            r = _mm(xr, bf(rwkv_w_rkv[idx, 0]))
            k = _mm(xk, bf(rwkv_w_rkv[idx, 1]))
            v = _mm(xv, bf(rwkv_w_rkv[idx, 2]))
            wl = _mm(_mm(xw, bf(_pad_cols(rwkv_w1[idx], 128)), act="tanh", out_dtype=BF16),
                     bf(_pad_rows(rwkv_w2[idx], 128)))
            al = _mm(_mm(xa, bf(_pad_cols(rwkv_a1[idx], 128)), out_dtype=BF16),
                     bf(_pad_rows(rwkv_a2[idx], 128)))
            gg = _mm(_mm(xg, bf(rwkv_g1[idx]), act="sigmoid", out_dtype=BF16), bf(rwkv_g2[idx]))
            if idx == 0:
                vmix = None
                v_first = v
            else:
                vl = _mm(_mm(xv, bf(_pad_cols(rwkv_v1[idx - 1], 128)), out_dtype=BF16),
                         bf(_pad_rows(rwkv_v2[idx - 1], 128)))
                vmix = (v_first, vl, rwkv_v0[idx - 1])
            params = (rwkv_w0[idx], rwkv_a0[idx], rwkv_k_k[idx], rwkv_k_a[idx],
                      rwkv_r_k[idx].reshape(d), rwkv_ln_w[idx], rwkv_ln_b[idx])
            yg = _rwkv_scan(r, k, v, wl, al, gg, vmix, params, bsz, seq)
            xf = _mm_post(yg, bf(rwkv_w_o[idx]), xf, g_m, norm_mix_post[layer], seq, tk=1024)
        else:
            (h,) = _prenorm(xf, norm_mix_pre[layer], sc_m, sh_m, None, seq)
            w_in = ssm_w_in[idx]
            z = _mm(h, bf(w_in[:, :SSM_INNER]))
            xbc = _mm_conv(h, [bf(w_in[:, SSM_INNER:SSM_INNER + SSM_CONV_DIM])], [ssm_conv_w[idx]],
                           [ssm_conv_b[idx]], seq, out_dtype=F32)
            dt = _mm(h, bf(_pad_cols(w_in[:, SSM_INNER + SSM_CONV_DIM:], 128)))
            yz, ssq = _ssd_scan(xbc, dt, z, ssm_dt_bias[idx], ssm_a_log[idx], ssm_d[idx], ssm_norm[idx],
                                bsz, seq)
            xf = _mm_post(yz, bf(ssm_w_out[idx]), xf, g_m, norm_mix_post[layer], seq,
                          ssq=ssq, ssq_dim=SSM_INNER, tk=1024)
        (h,) = _prenorm(xf, norm_ffn_pre[layer], sc_f, sh_f, None, seq)
        w_in = ffn_w_in[layer]
        cw = ffn_conv_w[layer]
        cb = ffn_conv_b[layer]
        F = FFN_HIDDEN
        act = _mm_conv(h,
                       [bf(_pad_cols(w_in[:, :F], FFN_PAD)), bf(_pad_cols(w_in[:, F:], FFN_PAD))],
                       [_pad_cols(cw[:, :F], FFN_PAD), _pad_cols(cw[:, F:], FFN_PAD)],
                       [jnp.pad(cb[:F], (0, FFN_PAD - F)), jnp.pad(cb[F:], (0, FFN_PAD - F))],
                       seq, out_dtype=BF16)
        xf = _mm_post(act, bf(_pad_rows(ffn_w_out[layer], FFN_PAD)), xf, g_f, norm_ffn_post[layer], seq, tk=512)
    return xf.reshape(bsz, seq, d)
```

```python
import functools

import jax
import jax.numpy as jnp
from jax import lax
from jax.experimental import pallas as pl
from jax.experimental.pallas import tpu as pltpu

F32 = jnp.float32
BF16 = jnp.bfloat16

D_MODEL = 2048
N_MOD = 6
RMS_EPS = 1e-6
GN_EPS = 64e-5
HEAD = 64
CHUNK = 64
RWKV_LANES = 256
RWKV_HB = RWKV_LANES // HEAD
RWKV_GROUPS_PER_STEP = 2
SSM_INNER = 2 * D_MODEL
SSM_GROUPS = 8
SSM_HPG = 8
SSM_STATE = 128
SSM_GL = SSM_HPG * HEAD
SSM_BC = SSM_GROUPS * SSM_STATE
SSM_CONV_DIM = SSM_INNER + 2 * SSM_BC
SSM_HEADS = SSM_GROUPS * SSM_HPG
FFN_HIDDEN = 5504
FFN_PAD = 5632
HALO = 16
VMEM_LIMIT = 56 * 1024 * 1024

NT_DIMS = (((1,), (1,)), ((), ()))
TN_DIMS = (((0,), (0,)), ((), ()))


def _cparams(sem):
    return pltpu.CompilerParams(dimension_semantics=sem, vmem_limit_bytes=VMEM_LIMIT)


def _bdot(a, b):
    return jnp.dot(a, b, preferred_element_type=F32)


def _split3(x):
    hi = x.astype(BF16)
    r1 = x - hi.astype(F32)
    mid = r1.astype(BF16)
    lo = (r1 - mid.astype(F32)).astype(BF16)
    return hi, mid, lo


def _sel_dot_l(m01, x):
    return sum(_bdot(m01, p) for p in _split3(x))


def _sel_dot_r(x, m01):
    return sum(_bdot(p, m01) for p in _split3(x))


def _softplus(z):
    return jnp.maximum(z, 0.0) + jnp.log1p(jnp.exp(-jnp.abs(z)))


def _silu(z):
    return z * jax.nn.sigmoid(z)


def _chunk_masks(tc):
    rt = lax.broadcasted_iota(jnp.int32, (tc, tc), 0)
    ct = lax.broadcasted_iota(jnp.int32, (tc, tc), 1)
    same = (rt // CHUNK) == (ct // CHUNK)
    ones = jnp.where(same, 1.0, 0.0).astype(BF16)
    tri = jnp.where(same & (ct <= rt), 1.0, 0.0).astype(BF16)
    return ones, tri


def _prenorm_kernel(x_ref, xh_ref, g_ref, sc_ref, sh_ref, *rest, n_mix, blocks_per_seq):
    def norm(xv):
        ms = jnp.mean(xv * xv, axis=-1, keepdims=True)
        return xv * lax.rsqrt(ms + RMS_EPS) * g_ref[...] * (1.0 + sc_ref[...]) + sh_ref[...]

    h = norm(x_ref[...])
    if n_mix == 0:
        rest[0][...] = h.astype(rest[0].dtype)
        return
    mu_ref, o_refs = rest[0], rest[1:]
    prev = norm(xh_ref[...])[HALO - 1:HALO, :]
    first = (pl.program_id(0) % blocks_per_seq) == 0
    prev = jnp.where(first, 0.0, prev)
    row = lax.broadcasted_iota(jnp.int32, h.shape, 0)
    hprev = jnp.where(row == 0, prev, pltpu.roll(h, 1, axis=0))
    xx = hprev - h
    for m in range(n_mix):
        o_refs[m][...] = (h + xx * mu_ref[m:m + 1, :]).astype(o_refs[m].dtype)


def _prenorm(x, g, sc, sh, mu, seq, tm=256):
    n, d = x.shape
    nb = seq // tm
    n_mix = 0 if mu is None else mu.shape[0]
    row = lambda i: (i, 0)
    halo = lambda i: (jnp.maximum(i * (tm // HALO) - 1, 0), 0)
    per_b = lambda i: (i // nb, 0, 0)
    in_specs = [pl.BlockSpec((tm, d), row), pl.BlockSpec((HALO, d), halo),
                pl.BlockSpec((1, d), lambda i: (0, 0)),
                pl.BlockSpec((None, 1, d), per_b), pl.BlockSpec((None, 1, d), per_b)]
    args = [x, x, g.reshape(1, d), sc[:, None, :], sh[:, None, :]]
    if n_mix:
        in_specs.append(pl.BlockSpec((n_mix, d), lambda i: (0, 0)))
        args.append(mu)
    n_out = max(n_mix, 1)
    out = pl.pallas_call(
        functools.partial(_prenorm_kernel, n_mix=n_mix, blocks_per_seq=nb),
        grid=(n // tm,),
        in_specs=in_specs,
        out_specs=[pl.BlockSpec((tm, d), row)] * n_out,
        out_shape=[jax.ShapeDtypeStruct((n, d), BF16)] * n_out,
        compiler_params=_cparams(("parallel",)),
        name="prenorm",
    )(*args)
    return out


def _mm_kernel(x_ref, w_ref, *rest, act, has_bias):
    o_ref = rest[-1]
    acc = jnp.dot(x_ref[...], w_ref[...], preferred_element_type=F32)
    if has_bias:
        acc = acc + rest[0][...]
    if act == "tanh":
        acc = jnp.tanh(acc)
    elif act == "sigmoid":
        acc = jax.nn.sigmoid(acc)
    o_ref[...] = acc.astype(o_ref.dtype)


def _mm(x, w, *, act=None, bias=None, out_dtype=F32, tm=1024, tn=1024):
    m, k = x.shape
    n = w.shape[1]
    tm = min(tm, m)
    tn = min(tn, n)
    in_specs = [pl.BlockSpec((tm, k), lambda i, j: (i, 0)), pl.BlockSpec((k, tn), lambda i, j: (0, j))]
    args = [x, w]
    if bias is not None:
        in_specs.append(pl.BlockSpec((1, tn), lambda i, j: (0, j)))
        args.append(bias.reshape(1, n))
    return pl.pallas_call(
        functools.partial(_mm_kernel, act=act, has_bias=bias is not None),
        grid=(m // tm, n // tn),
        in_specs=in_specs,
        out_specs=pl.BlockSpec((tm, tn), lambda i, j: (i, j)),
        out_shape=jax.ShapeDtypeStruct((m, n), out_dtype),
        compiler_params=_cparams(("parallel", "parallel")),
        name="mm",
    )(*args)


def _mm_conv_kernel(x_ref, xh_ref, *rest, taps, n_branch, blocks_per_seq, tm):
    w_refs = rest[:n_branch]
    cw_refs = rest[n_branch:2 * n_branch]
    cb_refs = rest[2 * n_branch:3 * n_branch]
    o_ref = rest[3 * n_branch]
    u_ref = rest[3 * n_branch + 1]
    first = (pl.program_id(0) % blocks_per_seq) == 0
    xh = xh_ref[...]
    xh = jnp.where(first, jnp.zeros_like(xh), xh)
    x = x_ref[...]

    def branch(b):
        w = w_refs[b][...]
        u_ref[0:HALO, :] = jnp.dot(xh, w, preferred_element_type=F32)
        u_ref[HALO:, :] = jnp.dot(x, w, preferred_element_type=F32)
        acc = cb_refs[b][...]
        for j in range(taps):
            acc = acc + cw_refs[b][j:j + 1, :] * u_ref[pl.ds(HALO - (taps - 1) + j, tm), :]
        return acc

    y = _silu(branch(0))
    if n_branch == 2:
        y = y * branch(1)
    o_ref[...] = y.astype(o_ref.dtype)


def _mm_conv(x, ws, cws, cbs, seq, *, out_dtype, tm=1024, tn=512):
    m, k = x.shape
    n = ws[0].shape[1]
    taps = cws[0].shape[0]
    tm = min(tm, seq)
    nb = seq // tm
    nbr = len(ws)
    col = lambda i, j: (0, j)
    in_specs = ([pl.BlockSpec((tm, k), lambda i, j: (i, 0)),
                 pl.BlockSpec((HALO, k), lambda i, j: (jnp.maximum(i * (tm // HALO) - 1, 0), 0))]
                + [pl.BlockSpec((k, tn), col)] * nbr
                + [pl.BlockSpec((taps, tn), col)] * nbr
                + [pl.BlockSpec((1, tn), col)] * nbr)
    return pl.pallas_call(
        functools.partial(_mm_conv_kernel, taps=taps, n_branch=nbr, blocks_per_seq=nb, tm=tm),
        grid=(m // tm, n // tn),
        in_specs=in_specs,
        out_specs=pl.BlockSpec((tm, tn), lambda i, j: (i, j)),
        out_shape=jax.ShapeDtypeStruct((m, n), out_dtype),
        scratch_shapes=[pltpu.VMEM((HALO + tm, tn), F32)],
        compiler_params=_cparams(("parallel", "parallel")),
        name="mm_conv",
    )(x, x, *ws, *cws, *[b.reshape(1, n) for b in cbs])


def _mm_post_kernel(a_ref, w_ref, res_ref, gate_ref, g_ref, *rest, nk, ssq_dim):
    if ssq_dim:
        ssq_ref, o_ref, acc_ref = rest
    else:
        o_ref, acc_ref = rest
    kk = pl.program_id(1)

    @pl.when(kk == 0)
    def _():
        acc_ref[...] = jnp.zeros_like(acc_ref)

    acc_ref[...] += jnp.dot(a_ref[...], w_ref[...], preferred_element_type=F32)

    @pl.when(kk == nk - 1)
    def _():
        f = acc_ref[...]
        if ssq_dim:
            f = f * lax.rsqrt(ssq_ref[:, 0:1] * (1.0 / ssq_dim) + RMS_EPS)
        ms = jnp.mean(f * f, axis=-1, keepdims=True)
        y = f * lax.rsqrt(ms + RMS_EPS) * g_ref[...]
        o_ref[...] = res_ref[...] + gate_ref[...] * y


def _mm_post(a, w, res, gate, g, seq, *, ssq=None, ssq_dim=0, tm=512, tk=512):
    m, k = a.shape
    d = w.shape[1]
    tm = min(tm, seq)
    nk = k // tk
    nb = seq // tm
    row = lambda i, kk: (i, 0)
    in_specs = [pl.BlockSpec((tm, tk), lambda i, kk: (i, kk)),
                pl.BlockSpec((tk, d), lambda i, kk: (kk, 0)),
                pl.BlockSpec((tm, d), row),
                pl.BlockSpec((None, 1, d), lambda i, kk: (i // nb, 0, 0)),
                pl.BlockSpec((1, d), lambda i, kk: (0, 0))]
    args = [a, w, res, gate[:, None, :], g.reshape(1, d)]
    if ssq is not None:
        in_specs.append(pl.BlockSpec((tm, 128), row))
        args.append(ssq)
    return pl.pallas_call(
        functools.partial(_mm_post_kernel, nk=nk, ssq_dim=ssq_dim),
        grid=(m // tm, nk),
        in_specs=in_specs,
        out_specs=pl.BlockSpec((tm, d), row),
        out_shape=jax.ShapeDtypeStruct((m, d), F32),
        scratch_shapes=[pltpu.VMEM((tm, d), F32)],
        compiler_params=_cparams(("parallel", "arbitrary")),
        name="mm_post",
    )(*args)


def _rwkv_scan_kernel(*refs, tc, ng, has_vmix):
    it = iter(refs)
    r_ref, k_ref, v_ref, wl_ref, al_ref, g_ref = (next(it) for _ in range(6))
    if has_vmix:
        vf_ref, vl_ref, v0_ref = (next(it) for _ in range(3))
    w0_ref, a0_ref, kk_ref, ka_ref, rk_ref, lnw_ref, lnb_ref = (next(it) for _ in range(7))
    o_ref = next(it)
    s_ref = next(it)

    L = RWKV_LANES
    C = CHUNK
    NC = tc // C

    @pl.when(pl.program_id(2) == 0)
    def _():
        s_ref[...] = jnp.zeros_like(s_ref)

    ri = lax.broadcasted_iota(jnp.int32, (L, L), 0)
    ci = lax.broadcasted_iota(jnp.int32, (L, L), 1)
    bd_f = jnp.where((ri // HEAD) == (ci // HEAD), 1.0, 0.0).astype(F32)
    bd_b = bd_f.astype(BF16)
    bd_b2 = jnp.concatenate([bd_b, bd_b], axis=1)
    chunk_ones, chunk_tri = _chunk_masks(tc)
    t_i = lax.broadcasted_iota(jnp.int32, (C, L), 0)
    s_i = lax.broadcasted_iota(jnp.int32, (C, L), 1) % HEAD
    strict = s_i < t_i
    incl = s_i <= t_i
    eye_cat = jnp.where(s_i == t_i, 1.0, 0.0).astype(F32)

    def headsum(xv):
        return _sel_dot_r(xv, bd_b)

    def vstack(xb):
        t = jnp.concatenate([xb] * RWKV_HB, axis=0)
        return t * (bd_b if xb.shape[1] == L else bd_b2)

    pro = []
    for g in range(ng):
        ln = slice(g * L, (g + 1) * L)
        r = r_ref[:, ln]
        k0 = k_ref[:, ln]
        v = v_ref[:, ln]
        if has_vmix:
            v = v + (vf_ref[:, ln] - v) * jax.nn.sigmoid(v0_ref[:, ln] + vl_ref[:, ln])
        w = -_softplus(-(w0_ref[:, ln] + wl_ref[:, ln])) - 0.5
        a = jax.nn.sigmoid(a0_ref[:, ln] + al_ref[:, ln])
        kk = k0 * kk_ref[:, ln]
        kk = kk * lax.rsqrt(jnp.maximum(headsum(kk * kk), 1e-24))
        k = k0 * (1.0 + (a - 1.0) * ka_ref[:, ln])
        lw = -jnp.exp(w)
        a_ = -kk
        b_ = kk * a
        cum = _sel_dot_l(chunk_tri, lw)
        cend = _sel_dot_l(chunk_ones, lw)
        e_neg = jnp.exp(-cum)
        e_end = jnp.exp(cend - cum)
        Rt = r * jnp.exp(cum)
        pro.append(dict(
            r=r, k=k, v=v, Rt=Rt,
            At=(a_ * jnp.exp(cum - lw)).astype(BF16), Rtb=Rt.astype(BF16),
            Kt=(k * e_neg).astype(BF16), Bt=(b_ * e_neg).astype(BF16),
            Kh=(k * e_end).astype(BF16), Bh=(b_ * e_end).astype(BF16),
            vb=v.astype(BF16), g_end=jnp.exp(cend)))

    chains = [(g, c) for g in range(ng) for c in range(NC)]
    n = range(len(chains))
    sl = [slice(c * C, (c + 1) * C) for _, c in chains]
    P_ = [pro[g] for g, _ in chains]
    G = [lax.dot_general(jnp.concatenate([P_[i]["At"][sl[i]], P_[i]["Rtb"][sl[i]]], axis=0),
                         jnp.concatenate([vstack(P_[i]["Bt"][sl[i]]), vstack(P_[i]["Kt"][sl[i]])], axis=0),
                         NT_DIMS, preferred_element_type=F32) for i in n]
    A_ab = [jnp.where(strict, G[i][:C, :L], 0.0) for i in n]
    A_akrk = [jnp.concatenate([jnp.where(strict, G[i][:C, L:], 0.0),
                               jnp.where(incl, G[i][C:, L:], 0.0)], axis=0).astype(BF16) for i in n]
    A_rb = [jnp.where(incl, G[i][C:, :L], 0.0).astype(BF16) for i in n]
    AV = [_bdot(A_akrk[i], vstack(P_[i]["vb"][sl[i]])) for i in n]
    T = [eye_cat + A_ab[i] for i in n]
    Ab = [A_ab[i].astype(BF16) for i in n]
    P = [_bdot(Ab[i], vstack(Ab[i])) for i in n]
    for _ in range(4):
        Pb = [P[i].astype(BF16) for i in n]
        res = [_bdot(Pb[i], vstack(jnp.concatenate([T[i].astype(BF16), Pb[i]], axis=1))) for i in n]
        T = [T[i] + res[i][:, :L] for i in n]
        P = [res[i][:, L:] for i in n]
    T = [T[i] + _bdot(P[i].astype(BF16), vstack(T[i].astype(BF16))) for i in n]
    WUb = [_bdot(T[i].astype(BF16),
                 vstack(jnp.concatenate([P_[i]["At"][sl[i]], AV[i][:C].astype(BF16)], axis=1))).astype(BF16)
           for i in n]
    RY = [_bdot(A_rb[i], vstack(WUb[i])) for i in n]
    rhat = [(P_[i]["Rt"][sl[i]] + RY[i][:, :L]).astype(BF16) for i in n]
    yhat = [AV[i][C:] + RY[i][:, L:] for i in n]
    m2t = [(lax.dot_general(WUb[i][:, :L], P_[i]["Bh"][sl[i]], TN_DIMS, preferred_element_type=F32)
            * bd_f).astype(BF16) for i in n]
    dst = [lax.dot_general(jnp.concatenate([P_[i]["vb"][sl[i]], WUb[i][:, L:]], axis=0),
                           jnp.concatenate([P_[i]["Kh"][sl[i]], P_[i]["Bh"][sl[i]]], axis=0),
                           TN_DIMS, preferred_element_type=F32) * bd_f for i in n]

    S = [s_ref[g] for g in range(ng)]
    ys = [[None] * NC for _ in range(ng)]
    for c in range(NC):
        for g in range(ng):
            i = g * NC + c
            Sb = S[g].astype(BF16)
            ys[g][c] = lax.dot_general(rhat[i], Sb, NT_DIMS, preferred_element_type=F32) + yhat[i]
            S[g] = S[g] * pro[g]["g_end"][c * C:c * C + 1, :] + _bdot(Sb, m2t[i]) + dst[i]
    for g in range(ng):
        s_ref[g] = S[g]

    for g in range(ng):
        ln = slice(g * L, (g + 1) * L)
        y = jnp.concatenate(ys[g], axis=0)
        mean = headsum(y) * (1.0 / HEAD)
        yc = y - mean
        var = headsum(yc * yc) * (1.0 / HEAD)
        yn = yc * lax.rsqrt(var + GN_EPS) * lnw_ref[:, ln] + lnb_ref[:, ln]
        yn = yn + headsum(pro[g]["r"] * pro[g]["k"] * rk_ref[:, ln]) * pro[g]["v"]
        o_ref[:, ln] = (yn * g_ref[:, ln]).astype(o_ref.dtype)


def _rwkv_scan(r, k, v, wl, al, g, vmix, params, bsz, seq, tc=256):
    d = D_MODEL
    ng = RWKV_GROUPS_PER_STEP
    W = ng * RWKV_LANES
    to3 = lambda t: t.reshape(bsz, seq, d)
    blk = pl.BlockSpec((None, tc, W), lambda b, h, t: (b, t, h))
    pblk = pl.BlockSpec((1, W), lambda b, h, t: (0, h))
    args = [to3(t) for t in (r, k, v, wl, al, g)]
    in_specs = [blk] * 6
    if vmix is not None:
        args += [to3(vmix[0]), to3(vmix[1]), vmix[2].reshape(1, d)]
        in_specs += [blk, blk, pblk]
    args += [p.reshape(1, d) for p in params]
    in_specs += [pblk] * 7
    out = pl.pallas_call(
        functools.partial(_rwkv_scan_kernel, tc=tc, ng=ng, has_vmix=vmix is not None),
        grid=(bsz, d // W, seq // tc),
        in_specs=in_specs,
        out_specs=blk,
        out_shape=jax.ShapeDtypeStruct((bsz, seq, d), BF16),
        scratch_shapes=[pltpu.VMEM((ng, RWKV_LANES, RWKV_LANES), F32)],
        compiler_params=_cparams(("parallel", "parallel", "arbitrary")),
        name="rwkv_scan",
    )(*args)
    return out.reshape(bsz * seq, d)


def _ssd_kernel(x_ref, b_ref, c_ref, dt_ref, z_ref, dtb_ref, alog_ref, dsk_ref, nw_ref,
                o_ref, ssq_ref, s_ref, *, tc):
    C = CHUNK
    GL = SSM_GL
    NC = tc // C
    g = pl.program_id(2)

    @pl.when((pl.program_id(1) == 0))
    def _():
        s_ref[g] = jnp.zeros((SSM_STATE, GL), F32)

    ej = lax.broadcasted_iota(jnp.int32, (128, GL), 0)
    ec = lax.broadcasted_iota(jnp.int32, (128, GL), 1) // HEAD
    expand = jnp.where(ej == g * SSM_HPG + ec, 1.0, 0.0).astype(BF16)
    _, chunk_tri = _chunk_masks(tc)
    ones_cc = jnp.ones((C, C), BF16)
    t_i = lax.broadcasted_iota(jnp.int32, (C, GL), 0)
    s_i = lax.broadcasted_iota(jnp.int32, (C, GL), 1) % HEAD
    incl = s_i <= t_i
    eye_cat = jnp.where(s_i == t_i, 1.0, 0.0).astype(F32)
    ri = lax.broadcasted_iota(jnp.int32, (GL, GL), 0) // HEAD
    ci = lax.broadcasted_iota(jnp.int32, (GL, GL), 1) // HEAD
    bd_b = jnp.where(ri == ci, 1.0, 0.0).astype(BF16)

    dt = _softplus(dt_ref[...] + dtb_ref[...])
    a_neg = -jnp.exp(alog_ref[...])
    acs_n = _sel_dot_l(chunk_tri, dt * a_neg)
    dt_x = _sel_dot_r(dt, expand)
    acs = _sel_dot_r(acs_n, expand)
    dsk_x = _sel_dot_r(jnp.broadcast_to(dsk_ref[...], (8, 128)), expand)[0:1, :]

    x = x_ref[...].astype(F32)
    xdt = x * dt_x
    xdt_b = xdt.astype(BF16)
    e_acs = jnp.exp(acs)
    b_b = b_ref[...].astype(BF16)
    c_b = c_ref[...].astype(BF16)

    cs = range(NC)
    sl = [slice(c * C, (c + 1) * C) for c in cs]
    acs_end = [acs[c * C + C - 1:c * C + C, :] for c in cs]
    zrow = [_sel_dot_l(ones_cc, acs[sl[c]] * eye_cat) for c in cs]
    cb = [lax.dot_general(c_b[sl[c]], jnp.concatenate([b_b[sl[c]]] * SSM_HPG, axis=0), NT_DIMS,
                          preferred_element_type=F32) for c in cs]
    upd = [lax.dot_general(b_b[sl[c]], (xdt[sl[c]] * jnp.exp(acs_end[c] - acs[sl[c]])).astype(BF16),
                           TN_DIMS, preferred_element_type=F32) for c in cs]
    m = [(cb[c] * jnp.where(incl, jnp.exp(jnp.minimum(acs[sl[c]] - zrow[c], 0.0)), 0.0)).astype(BF16)
         for c in cs]
    y_in = [_bdot(m[c], jnp.concatenate([xdt_b[sl[c]]] * SSM_HPG, axis=0) * bd_b) for c in cs]
    S = s_ref[g]
    S_in = []
    for c in cs:
        S_in.append(S.astype(BF16))
        S = S * jnp.exp(acs_end[c]) + upd[c]
    s_ref[g] = S
    y_st = [_bdot(c_b[sl[c]], S_in[c]) for c in cs]
    y = jnp.concatenate([y_in[c] + y_st[c] * e_acs[sl[c]] for c in cs], axis=0) + x * dsk_x

    yz = y * _silu(z_ref[...])
    part = jnp.sum(yz * yz, axis=-1, keepdims=True)

    @pl.when(g == 0)
    def _():
        ssq_ref[...] = jnp.zeros_like(ssq_ref)

    ssq_ref[...] += jnp.broadcast_to(part, ssq_ref.shape)
    o_ref[...] = (yz * nw_ref[...]).astype(o_ref.dtype)


def _ssd_scan(xbc, dt, z, dt_bias, a_log, d_skip, norm_w, bsz, seq, tc=256):
    G = SSM_GROUPS
    GL = SSM_GL
    xbc3 = xbc.reshape(bsz, seq, SSM_CONV_DIM)
    dt3 = dt.reshape(bsz, seq, 128)
    z3 = z.reshape(bsz, seq, SSM_INNER)
    pad128 = lambda t: jnp.pad(t.reshape(1, SSM_HEADS), ((0, 0), (0, 128 - SSM_HEADS)))
    x_spec = pl.BlockSpec((None, tc, GL), lambda b, t, g: (b, t, g))
    b_spec = pl.BlockSpec((None, tc, SSM_STATE), lambda b, t, g: (b, t, SSM_INNER // SSM_STATE + g))
    c_spec = pl.BlockSpec((None, tc, SSM_STATE), lambda b, t, g: (b, t, (SSM_INNER + SSM_BC) // SSM_STATE + g))
    dt_spec = pl.BlockSpec((None, tc, 128), lambda b, t, g: (b, t, 0))
    p128 = pl.BlockSpec((1, 128), lambda b, t, g: (0, 0))
    out, ssq = pl.pallas_call(
        functools.partial(_ssd_kernel, tc=tc),
        grid=(bsz, seq // tc, G),
        in_specs=[x_spec, b_spec, c_spec, dt_spec, x_spec, p128, p128, p128,
                  pl.BlockSpec((1, GL), lambda b, t, g: (0, g))],
        out_specs=[x_spec, dt_spec],
        out_shape=[jax.ShapeDtypeStruct((bsz, seq, SSM_INNER), BF16),
                   jax.ShapeDtypeStruct((bsz, seq, 128), F32)],
        scratch_shapes=[pltpu.VMEM((G, SSM_STATE, GL), F32)],
        compiler_params=_cparams(("parallel", "arbitrary", "arbitrary")),
        name="ssd_scan",
    )(xbc3, xbc3, xbc3, dt3, z3, pad128(dt_bias), pad128(a_log), pad128(d_skip),
      norm_w.reshape(1, SSM_INNER))
    return out.reshape(bsz * seq, SSM_INNER), ssq.reshape(bsz * seq, 128)


def _pad_cols(w, n):
    return jnp.pad(w, ((0, 0), (0, n - w.shape[1])))


def _pad_rows(w, n):
    return jnp.pad(w, ((0, n - w.shape[0]), (0, 0)))


def kernel(x, c, ada_w, ada_b, ada_table, norm_mix_pre, norm_mix_post, norm_ffn_pre, norm_ffn_post, rwkv_mu, rwkv_w_rkv, rwkv_w0, rwkv_w1, rwkv_w2, rwkv_a0, rwkv_a1, rwkv_a2, rwkv_v0, rwkv_v1, rwkv_v2, rwkv_g1, rwkv_g2, rwkv_k_k, rwkv_k_a, rwkv_r_k, rwkv_ln_w, rwkv_ln_b, rwkv_w_o, ssm_w_in, ssm_conv_w, ssm_conv_b, ssm_dt_bias, ssm_a_log, ssm_d, ssm_norm, ssm_w_out, ffn_w_in, ffn_conv_w, ffn_conv_b, ffn_w_out):
    bsz, seq, d = x.shape
    depth = ada_table.shape[0]
    n = bsz * seq
    bf = lambda t: t.astype(BF16)

    sc_in = jnp.pad(bf(jax.nn.silu(c)), ((0, 16 - bsz), (0, 0)))
    mod = _mm(sc_in, bf(ada_w), bias=ada_b)[:bsz]

    xf = x.reshape(n, d)
    v_first = None
    for layer in range(depth):
        sh_m, sc_m, g_m, sh_f, sc_f, g_f = jnp.split(mod + ada_table[layer], N_MOD, axis=-1)
        idx = layer // 2
        if layer % 2 == 0:
            mu = rwkv_mu[idx]
            xr, xk, xv, xw, xa, xg = _prenorm(xf, norm_mix_pre[layer], sc_m, sh_m, mu, seq)
            r = _mm(xr, bf(rwkv_w_rkv[idx, 0]))
            k = _mm(xk, bf(rwkv_w_rkv[idx, 1]))
            v = _mm(xv, bf(rwkv_w_rkv[idx, 2]))
            wl = _mm(_mm(xw, bf(_pad_cols(rwkv_w1[idx], 128)), act="tanh", out_dtype=BF16),
                     bf(_pad_rows(rwkv_w2[idx], 128)))
            al = _mm(_mm(xa, bf(_pad_cols(rwkv_a1[idx], 128)), out_dtype=BF16),
                     bf(_pad_rows(rwkv_a2[idx], 128)))
            gg = _mm(_mm(xg, bf(rwkv_g1[idx]), act="sigmoid", out_dtype=BF16), bf(rwkv_g2[idx]))
            if idx == 0:
                vmix = None
                v_first = v
            else:
                vl = _mm(_mm(xv, bf(_pad_cols(rwkv_v1[idx - 1], 128)), out_dtype=BF16),
                         bf(_pad_rows(rwkv_v2[idx - 1], 128)))
                vmix = (v_first, vl, rwkv_v0[idx - 1])
            params = (rwkv_w0[idx], rwkv_a0[idx], rwkv_k_k[idx], rwkv_k_a[idx],
                      rwkv_r_k[idx].reshape(d), rwkv_ln_w[idx], rwkv_ln_b[idx])
            yg = _rwkv_scan(r, k, v, wl, al, gg, vmix, params, bsz, seq)
            xf = _mm_post(yg, bf(rwkv_w_o[idx]), xf, g_m, norm_mix_post[layer], seq, tk=1024)
        else:
            (h,) = _prenorm(xf, norm_mix_pre[layer], sc_m, sh_m, None, seq)
            w_in = ssm_w_in[idx]
            z = _mm(h, bf(w_in[:, :SSM_INNER]))
            xbc = _mm_conv(h, [bf(w_in[:, SSM_INNER:SSM_INNER + SSM_CONV_DIM])], [ssm_conv_w[idx]],
                           [ssm_conv_b[idx]], seq, out_dtype=F32)
            dt = _mm(h, bf(_pad_cols(w_in[:, SSM_INNER + SSM_CONV_DIM:], 128)))
            yz, ssq = _ssd_scan(xbc, dt, z, ssm_dt_bias[idx], ssm_a_log[idx], ssm_d[idx], ssm_norm[idx],
                                bsz, seq)
            xf = _mm_post(yz, bf(ssm_w_out[idx]), xf, g_m, norm_mix_post[layer], seq,
                          ssq=ssq, ssq_dim=SSM_INNER, tk=1024)
        (h,) = _prenorm(xf, norm_ffn_pre[layer], sc_f, sh_f, None, seq)
        w_in = ffn_w_in[layer]
        cw = ffn_conv_w[layer]
        cb = ffn_conv_b[layer]
        F = FFN_HIDDEN
        act = _mm_conv(h,
                       [bf(_pad_cols(w_in[:, :F], FFN_PAD)), bf(_pad_cols(w_in[:, F:], FFN_PAD))],
                       [_pad_cols(cw[:, :F], FFN_PAD), _pad_cols(cw[:, F:], FFN_PAD)],
                       [jnp.pad(cb[:F], (0, FFN_PAD - F)), jnp.pad(cb[F:], (0, FFN_PAD - F))],
                       seq, out_dtype=BF16)
        xf = _mm_post(act, bf(_pad_rows(ffn_w_out[layer], FFN_PAD)), xf, g_f, norm_ffn_post[layer], seq, tk=512)
    return xf.reshape(bsz, seq, d)
```

```python
import functools

import jax
import jax.numpy as jnp
from jax import lax
from jax.experimental import pallas as pl
from jax.experimental.pallas import tpu as pltpu

F32 = jnp.float32
BF16 = jnp.bfloat16

D_MODEL = 2048
N_MOD = 6
RMS_EPS = 1e-6
GN_EPS = 64e-5
HEAD = 64
CHUNK = 64
RWKV_LANES = 256
RWKV_HB = RWKV_LANES // HEAD
RWKV_GROUPS_PER_STEP = 2
SSM_INNER = 2 * D_MODEL
SSM_GROUPS = 8
SSM_HPG = 8
SSM_STATE = 128
SSM_GL = SSM_HPG * HEAD
SSM_BC = SSM_GROUPS * SSM_STATE
SSM_CONV_DIM = SSM_INNER + 2 * SSM_BC
SSM_HEADS = SSM_GROUPS * SSM_HPG
FFN_HIDDEN = 5504
FFN_PAD = 5632
HALO = 16
VMEM_LIMIT = 56 * 1024 * 1024

NT_DIMS = (((1,), (1,)), ((), ()))
TN_DIMS = (((0,), (0,)), ((), ()))


def _cparams(sem):
    return pltpu.CompilerParams(dimension_semantics=sem, vmem_limit_bytes=VMEM_LIMIT)


def _bdot(a, b):
    return jnp.dot(a, b, preferred_element_type=F32)


def _split3(x):
    hi = x.astype(BF16)
    r1 = x - hi.astype(F32)
    mid = r1.astype(BF16)
    lo = (r1 - mid.astype(F32)).astype(BF16)
    return hi, mid, lo


def _sel_dot_l(m01, x):
    return sum(_bdot(m01, p) for p in _split3(x))


def _sel_dot_r(x, m01):
    return sum(_bdot(p, m01) for p in _split3(x))


def _softplus(z):
    return jnp.maximum(z, 0.0) + jnp.log1p(jnp.exp(-jnp.abs(z)))


def _silu(z):
    return z * jax.nn.sigmoid(z)


def _chunk_masks(tc):
    rt = lax.broadcasted_iota(jnp.int32, (tc, tc), 0)
    ct = lax.broadcasted_iota(jnp.int32, (tc, tc), 1)
    same = (rt // CHUNK) == (ct // CHUNK)
    ones = jnp.where(same, 1.0, 0.0).astype(BF16)
    tri = jnp.where(same & (ct <= rt), 1.0, 0.0).astype(BF16)
    return ones, tri


def _prenorm_kernel(x_ref, xh_ref, g_ref, sc_ref, sh_ref, *rest, n_mix, blocks_per_seq):
    def norm(xv):
        ms = jnp.mean(xv * xv, axis=-1, keepdims=True)
        return xv * lax.rsqrt(ms + RMS_EPS) * g_ref[...] * (1.0 + sc_ref[...]) + sh_ref[...]

    h = norm(x_ref[...])
    if n_mix == 0:
        rest[0][...] = h.astype(rest[0].dtype)
        return
    mu_ref, o_refs = rest[0], rest[1:]
    prev = norm(xh_ref[...])[HALO - 1:HALO, :]
    first = (pl.program_id(0) % blocks_per_seq) == 0
    prev = jnp.where(first, 0.0, prev)
    row = lax.broadcasted_iota(jnp.int32, h.shape, 0)
    hprev = jnp.where(row == 0, prev, pltpu.roll(h, 1, axis=0))
    xx = hprev - h
    for m in range(n_mix):
        o_refs[m][...] = (h + xx * mu_ref[m:m + 1, :]).astype(o_refs[m].dtype)


def _prenorm(x, g, sc, sh, mu, seq, tm=256):
    n, d = x.shape
    nb = seq // tm
    n_mix = 0 if mu is None else mu.shape[0]
    row = lambda i: (i, 0)
    halo = lambda i: (jnp.maximum(i * (tm // HALO) - 1, 0), 0)
    per_b = lambda i: (i // nb, 0, 0)
    in_specs = [pl.BlockSpec((tm, d), row), pl.BlockSpec((HALO, d), halo),
                pl.BlockSpec((1, d), lambda i: (0, 0)),
                pl.BlockSpec((None, 1, d), per_b), pl.BlockSpec((None, 1, d), per_b)]
    args = [x, x, g.reshape(1, d), sc[:, None, :], sh[:, None, :]]
    if n_mix:
        in_specs.append(pl.BlockSpec((n_mix, d), lambda i: (0, 0)))
        args.append(mu)
    n_out = max(n_mix, 1)
    out = pl.pallas_call(
        functools.partial(_prenorm_kernel, n_mix=n_mix, blocks_per_seq=nb),
        grid=(n // tm,),
        in_specs=in_specs,
        out_specs=[pl.BlockSpec((tm, d), row)] * n_out,
        out_shape=[jax.ShapeDtypeStruct((n, d), BF16)] * n_out,
        compiler_params=_cparams(("parallel",)),
        name="prenorm",
    )(*args)
    return out


def _mm_kernel(x_ref, w_ref, *rest, act, has_bias):
    o_ref = rest[-1]
    acc = jnp.dot(x_ref[...], w_ref[...], preferred_element_type=F32)
    if has_bias:
        acc = acc + rest[0][...]
    if act == "tanh":
        acc = jnp.tanh(acc)
    elif act == "sigmoid":
        acc = jax.nn.sigmoid(acc)
    o_ref[...] = acc.astype(o_ref.dtype)


def _mm(x, w, *, act=None, bias=None, out_dtype=F32, tm=1024, tn=1024):
    m, k = x.shape
    n = w.shape[1]
    tm = min(tm, m)
    tn = min(tn, n)
    in_specs = [pl.BlockSpec((tm, k), lambda i, j: (i, 0)), pl.BlockSpec((k, tn), lambda i, j: (0, j))]
    args = [x, w]
    if bias is not None:
        in_specs.append(pl.BlockSpec((1, tn), lambda i, j: (0, j)))
        args.append(bias.reshape(1, n))
    return pl.pallas_call(
        functools.partial(_mm_kernel, act=act, has_bias=bias is not None),
        grid=(m // tm, n // tn),
        in_specs=in_specs,
        out_specs=pl.BlockSpec((tm, tn), lambda i, j: (i, j)),
        out_shape=jax.ShapeDtypeStruct((m, n), out_dtype),
        compiler_params=_cparams(("parallel", "parallel")),
        name="mm",
    )(*args)


def _mm_conv_kernel(x_ref, xh_ref, *rest, taps, n_branch, blocks_per_seq, tm):
    w_refs = rest[:n_branch]
    cw_refs = rest[n_branch:2 * n_branch]
    cb_refs = rest[2 * n_branch:3 * n_branch]
    o_ref = rest[3 * n_branch]
    u_ref = rest[3 * n_branch + 1]
    first = (pl.program_id(0) % blocks_per_seq) == 0
    xh = xh_ref[...]
    xh = jnp.where(first, jnp.zeros_like(xh), xh)
    x = x_ref[...]

    def branch(b):
        w = w_refs[b][...]
        u_ref[0:HALO, :] = jnp.dot(xh, w, preferred_element_type=F32)
        u_ref[HALO:, :] = jnp.dot(x, w, preferred_element_type=F32)
        acc = cb_refs[b][...]
        for j in range(taps):
            acc = acc + cw_refs[b][j:j + 1, :] * u_ref[pl.ds(HALO - (taps - 1) + j, tm), :]
        return acc

    y = _silu(branch(0))
    if n_branch == 2:
        y = y * branch(1)
    o_ref[...] = y.astype(o_ref.dtype)


def _mm_conv(x, ws, cws, cbs, seq, *, out_dtype, tm=1024, tn=512):
    m, k = x.shape
    n = ws[0].shape[1]
    taps = cws[0].shape[0]
    tm = min(tm, seq)
    nb = seq // tm
    nbr = len(ws)
    col = lambda i, j: (0, j)
    in_specs = ([pl.BlockSpec((tm, k), lambda i, j: (i, 0)),
                 pl.BlockSpec((HALO, k), lambda i, j: (jnp.maximum(i * (tm // HALO) - 1, 0), 0))]
                + [pl.BlockSpec((k, tn), col)] * nbr
                + [pl.BlockSpec((taps, tn), col)] * nbr
                + [pl.BlockSpec((1, tn), col)] * nbr)
    return pl.pallas_call(
        functools.partial(_mm_conv_kernel, taps=taps, n_branch=nbr, blocks_per_seq=nb, tm=tm),
        grid=(m // tm, n // tn),
        in_specs=in_specs,
        out_specs=pl.BlockSpec((tm, tn), lambda i, j: (i, j)),
        out_shape=jax.ShapeDtypeStruct((m, n), out_dtype),
        scratch_shapes=[pltpu.VMEM((HALO + tm, tn), F32)],
        compiler_params=_cparams(("parallel", "parallel")),
        name="mm_conv",
    )(x, x, *ws, *cws, *[b.reshape(1, n) for b in cbs])


def _mm_post_kernel(a_ref, w_ref, res_ref, gate_ref, g_ref, *rest, nk, ssq_dim):
    if ssq_dim:
        ssq_ref, o_ref, acc_ref = rest
    else:
        o_ref, acc_ref = rest
    kk = pl.program_id(1)

    @pl.when(kk == 0)
    def _():
        acc_ref[...] = jnp.zeros_like(acc_ref)

    acc_ref[...] += jnp.dot(a_ref[...], w_ref[...], preferred_element_type=F32)

    @pl.when(kk == nk - 1)
    def _():
        f = acc_ref[...]
        if ssq_dim:
            f = f * lax.rsqrt(ssq_ref[:, 0:1] * (1.0 / ssq_dim) + RMS_EPS)
        ms = jnp.mean(f * f, axis=-1, keepdims=True)
        y = f * lax.rsqrt(ms + RMS_EPS) * g_ref[...]
        o_ref[...] = res_ref[...] + gate_ref[...] * y


def _mm_post(a, w, res, gate, g, seq, *, ssq=None, ssq_dim=0, tm=1024, tk=512):
    m, k = a.shape
    d = w.shape[1]
    tm = min(tm, seq)
    nk = k // tk
    nb = seq // tm
    row = lambda i, kk: (i, 0)
    in_specs = [pl.BlockSpec((tm, tk), lambda i, kk: (i, kk)),
                pl.BlockSpec((tk, d), lambda i, kk: (kk, 0)),
                pl.BlockSpec((tm, d), row),
                pl.BlockSpec((None, 1, d), lambda i, kk: (i // nb, 0, 0)),
                pl.BlockSpec((1, d), lambda i, kk: (0, 0))]
    args = [a, w, res, gate[:, None, :], g.reshape(1, d)]
    if ssq is not None:
        in_specs.append(pl.BlockSpec((tm, 128), row))
        args.append(ssq)
    return pl.pallas_call(
        functools.partial(_mm_post_kernel, nk=nk, ssq_dim=ssq_dim),
        grid=(m // tm, nk),
        in_specs=in_specs,
        out_specs=pl.BlockSpec((tm, d), row),
        out_shape=jax.ShapeDtypeStruct((m, d), F32),
        scratch_shapes=[pltpu.VMEM((tm, d), F32)],
        compiler_params=_cparams(("parallel", "arbitrary")),
        name="mm_post",
    )(*args)


def _rwkv_scan_kernel(*refs, tc, ng, has_vmix):
    it = iter(refs)
    r_ref, k_ref, v_ref, wl_ref, al_ref, g_ref = (next(it) for _ in range(6))
    if has_vmix:
        vf_ref, vl_ref, v0_ref = (next(it) for _ in range(3))
    w0_ref, a0_ref, kk_ref, ka_ref, rk_ref, lnw_ref, lnb_ref = (next(it) for _ in range(7))
    o_ref = next(it)
    s_ref = next(it)

    L = RWKV_LANES
    C = CHUNK
    NC = tc // C

    @pl.when(pl.program_id(2) == 0)
    def _():
        s_ref[...] = jnp.zeros_like(s_ref)

    ri = lax.broadcasted_iota(jnp.int32, (L, L), 0)
    ci = lax.broadcasted_iota(jnp.int32, (L, L), 1)
    bd_f = jnp.where((ri // HEAD) == (ci // HEAD), 1.0, 0.0).astype(F32)
    bd_b = bd_f.astype(BF16)
    bd_b2 = jnp.concatenate([bd_b, bd_b], axis=1)
    chunk_ones, chunk_tri = _chunk_masks(tc)
    t_i = lax.broadcasted_iota(jnp.int32, (C, L), 0)
    s_i = lax.broadcasted_iota(jnp.int32, (C, L), 1) % HEAD
    strict = s_i < t_i
    incl = s_i <= t_i
    eye_cat = jnp.where(s_i == t_i, 1.0, 0.0).astype(F32)

    def headsum(xv):
        return _sel_dot_r(xv, bd_b)

    def vstack(xb):
        t = jnp.concatenate([xb] * RWKV_HB, axis=0)
        return t * (bd_b if xb.shape[1] == L else bd_b2)

    pro = []
    for g in range(ng):
        ln = slice(g * L, (g + 1) * L)
        r = r_ref[:, ln]
        k0 = k_ref[:, ln]
        v = v_ref[:, ln]
        if has_vmix:
            v = v + (vf_ref[:, ln] - v) * jax.nn.sigmoid(v0_ref[:, ln] + vl_ref[:, ln])
        w = -_softplus(-(w0_ref[:, ln] + wl_ref[:, ln])) - 0.5
        a = jax.nn.sigmoid(a0_ref[:, ln] + al_ref[:, ln])
        kk = k0 * kk_ref[:, ln]
        kk = kk * lax.rsqrt(jnp.maximum(headsum(kk * kk), 1e-24))
        k = k0 * (1.0 + (a - 1.0) * ka_ref[:, ln])
        lw = -jnp.exp(w)
        a_ = -kk
        b_ = kk * a
        cum = _sel_dot_l(chunk_tri, lw)
        cend = _sel_dot_l(chunk_ones, lw)
        e_neg = jnp.exp(-cum)
        e_end = jnp.exp(cend - cum)
        Rt = r * jnp.exp(cum)
        pro.append(dict(
            r=r, k=k, v=v, Rt=Rt,
            At=(a_ * jnp.exp(cum - lw)).astype(BF16), Rtb=Rt.astype(BF16),
            Kt=(k * e_neg).astype(BF16), Bt=(b_ * e_neg).astype(BF16),
            Kh=(k * e_end).astype(BF16), Bh=(b_ * e_end).astype(BF16),
            vb=v.astype(BF16), g_end=jnp.exp(cend)))

    chains = [(g, c) for g in range(ng) for c in range(NC)]
    n = range(len(chains))
    sl = [slice(c * C, (c + 1) * C) for _, c in chains]
    P_ = [pro[g] for g, _ in chains]
    G = [lax.dot_general(jnp.concatenate([P_[i]["At"][sl[i]], P_[i]["Rtb"][sl[i]]], axis=0),
                         jnp.concatenate([vstack(P_[i]["Bt"][sl[i]]), vstack(P_[i]["Kt"][sl[i]])], axis=0),
                         NT_DIMS, preferred_element_type=F32) for i in n]
    A_ab = [jnp.where(strict, G[i][:C, :L], 0.0) for i in n]
    A_akrk = [jnp.concatenate([jnp.where(strict, G[i][:C, L:], 0.0),
                               jnp.where(incl, G[i][C:, L:], 0.0)], axis=0).astype(BF16) for i in n]
    A_rb = [jnp.where(incl, G[i][C:, :L], 0.0).astype(BF16) for i in n]
    AV = [_bdot(A_akrk[i], vstack(P_[i]["vb"][sl[i]])) for i in n]
    T = [eye_cat + A_ab[i] for i in n]
    Ab = [A_ab[i].astype(BF16) for i in n]
    P = [_bdot(Ab[i], vstack(Ab[i])) for i in n]
    for _ in range(4):
        Pb = [P[i].astype(BF16) for i in n]
        res = [_bdot(Pb[i], vstack(jnp.concatenate([T[i].astype(BF16), Pb[i]], axis=1))) for i in n]
        T = [T[i] + res[i][:, :L] for i in n]
        P = [res[i][:, L:] for i in n]
    T = [T[i] + _bdot(P[i].astype(BF16), vstack(T[i].astype(BF16))) for i in n]
    WUb = [_bdot(T[i].astype(BF16),
                 vstack(jnp.concatenate([P_[i]["At"][sl[i]], AV[i][:C].astype(BF16)], axis=1))).astype(BF16)
           for i in n]
    RY = [_bdot(A_rb[i], vstack(WUb[i])) for i in n]
    rhat = [(P_[i]["Rt"][sl[i]] + RY[i][:, :L]).astype(BF16) for i in n]
    yhat = [AV[i][C:] + RY[i][:, L:] for i in n]
    m2t = [(lax.dot_general(WUb[i][:, :L], P_[i]["Bh"][sl[i]], TN_DIMS, preferred_element_type=F32)
            * bd_f).astype(BF16) for i in n]
    dst = [lax.dot_general(jnp.concatenate([P_[i]["vb"][sl[i]], WUb[i][:, L:]], axis=0),
                           jnp.concatenate([P_[i]["Kh"][sl[i]], P_[i]["Bh"][sl[i]]], axis=0),
                           TN_DIMS, preferred_element_type=F32) * bd_f for i in n]

    S = [s_ref[g] for g in range(ng)]
    ys = [[None] * NC for _ in range(ng)]
    for c in range(NC):
        for g in range(ng):
            i = g * NC + c
            Sb = S[g].astype(BF16)
            ys[g][c] = lax.dot_general(rhat[i], Sb, NT_DIMS, preferred_element_type=F32) + yhat[i]
            S[g] = S[g] * pro[g]["g_end"][c * C:c * C + 1, :] + _bdot(Sb, m2t[i]) + dst[i]
    for g in range(ng):
        s_ref[g] = S[g]

    for g in range(ng):
        ln = slice(g * L, (g + 1) * L)
        y = jnp.concatenate(ys[g], axis=0)
        mean = headsum(y) * (1.0 / HEAD)
        yc = y - mean
        var = headsum(yc * yc) * (1.0 / HEAD)
        yn = yc * lax.rsqrt(var + GN_EPS) * lnw_ref[:, ln] + lnb_ref[:, ln]
        yn = yn + headsum(pro[g]["r"] * pro[g]["k"] * rk_ref[:, ln]) * pro[g]["v"]
        o_ref[:, ln] = (yn * g_ref[:, ln]).astype(o_ref.dtype)


def _rwkv_scan(r, k, v, wl, al, g, vmix, params, bsz, seq, tc=256):
    d = D_MODEL
    ng = RWKV_GROUPS_PER_STEP
    W = ng * RWKV_LANES
    to3 = lambda t: t.reshape(bsz, seq, d)
    blk = pl.BlockSpec((None, tc, W), lambda b, h, t: (b, t, h))
    pblk = pl.BlockSpec((1, W), lambda b, h, t: (0, h))
    args = [to3(t) for t in (r, k, v, wl, al, g)]
    in_specs = [blk] * 6
    if vmix is not None:
        args += [to3(vmix[0]), to3(vmix[1]), vmix[2].reshape(1, d)]
        in_specs += [blk, blk, pblk]
    args += [p.reshape(1, d) for p in params]
    in_specs += [pblk] * 7
    out = pl.pallas_call(
        functools.partial(_rwkv_scan_kernel, tc=tc, ng=ng, has_vmix=vmix is not None),
        grid=(bsz, d // W, seq // tc),
        in_specs=in_specs,
        out_specs=blk,
        out_shape=jax.ShapeDtypeStruct((bsz, seq, d), BF16),
        scratch_shapes=[pltpu.VMEM((ng, RWKV_LANES, RWKV_LANES), F32)],
        compiler_params=_cparams(("parallel", "parallel", "arbitrary")),
        name="rwkv_scan",
    )(*args)
    return out.reshape(bsz * seq, d)


def _ssd_kernel(x_ref, b_ref, c_ref, dt_ref, z_ref, dtb_ref, alog_ref, dsk_ref, nw_ref,
                o_ref, ssq_ref, s_ref, *, tc):
    C = CHUNK
    GL = SSM_GL
    NC = tc // C
    g = pl.program_id(2)

    @pl.when((pl.program_id(1) == 0))
    def _():
        s_ref[g] = jnp.zeros((SSM_STATE, GL), F32)

    ej = lax.broadcasted_iota(jnp.int32, (128, GL), 0)
    ec = lax.broadcasted_iota(jnp.int32, (128, GL), 1) // HEAD
    expand = jnp.where(ej == g * SSM_HPG + ec, 1.0, 0.0).astype(BF16)
    _, chunk_tri = _chunk_masks(tc)
    ones_cc = jnp.ones((C, C), BF16)
    t_i = lax.broadcasted_iota(jnp.int32, (C, GL), 0)
    s_i = lax.broadcasted_iota(jnp.int32, (C, GL), 1) % HEAD
    incl = s_i <= t_i
    eye_cat = jnp.where(s_i == t_i, 1.0, 0.0).astype(F32)
    ri = lax.broadcasted_iota(jnp.int32, (GL, GL), 0) // HEAD
    ci = lax.broadcasted_iota(jnp.int32, (GL, GL), 1) // HEAD
    bd_b = jnp.where(ri == ci, 1.0, 0.0).astype(BF16)

    dt = _softplus(dt_ref[...] + dtb_ref[...])
    a_neg = -jnp.exp(alog_ref[...])
    acs_n = _sel_dot_l(chunk_tri, dt * a_neg)
    dt_x = _sel_dot_r(dt, expand)
    acs = _sel_dot_r(acs_n, expand)
    dsk_x = _sel_dot_r(jnp.broadcast_to(dsk_ref[...], (8, 128)), expand)[0:1, :]

    x = x_ref[...].astype(F32)
    xdt = x * dt_x
    xdt_b = xdt.astype(BF16)
    e_acs = jnp.exp(acs)
    b_b = b_ref[...].astype(BF16)
    c_b = c_ref[...].astype(BF16)

    cs = range(NC)
    sl = [slice(c * C, (c + 1) * C) for c in cs]
    acs_end = [acs[c * C + C - 1:c * C + C, :] for c in cs]
    zrow = [_sel_dot_l(ones_cc, acs[sl[c]] * eye_cat) for c in cs]
    cb = [lax.dot_general(c_b[sl[c]], jnp.concatenate([b_b[sl[c]]] * SSM_HPG, axis=0), NT_DIMS,
                          preferred_element_type=F32) for c in cs]
    upd = [lax.dot_general(b_b[sl[c]], (xdt[sl[c]] * jnp.exp(acs_end[c] - acs[sl[c]])).astype(BF16),
                           TN_DIMS, preferred_element_type=F32) for c in cs]
    m = [(cb[c] * jnp.where(incl, jnp.exp(jnp.minimum(acs[sl[c]] - zrow[c], 0.0)), 0.0)).astype(BF16)
         for c in cs]
    y_in = [_bdot(m[c], jnp.concatenate([xdt_b[sl[c]]] * SSM_HPG, axis=0) * bd_b) for c in cs]
    S = s_ref[g]
    S_in = []
    for c in cs:
        S_in.append(S.astype(BF16))
        S = S * jnp.exp(acs_end[c]) + upd[c]
    s_ref[g] = S
    y_st = [_bdot(c_b[sl[c]], S_in[c]) for c in cs]
    y = jnp.concatenate([y_in[c] + y_st[c] * e_acs[sl[c]] for c in cs], axis=0) + x * dsk_x

    yz = y * _silu(z_ref[...])
    part = jnp.sum(yz * yz, axis=-1, keepdims=True)

    @pl.when(g == 0)
    def _():
        ssq_ref[...] = jnp.zeros_like(ssq_ref)

    ssq_ref[...] += jnp.broadcast_to(part, ssq_ref.shape)
    o_ref[...] = (yz * nw_ref[...]).astype(o_ref.dtype)


def _ssd_scan(xbc, dt, z, dt_bias, a_log, d_skip, norm_w, bsz, seq, tc=256):
    G = SSM_GROUPS
    GL = SSM_GL
    xbc3 = xbc.reshape(bsz, seq, SSM_CONV_DIM)
    dt3 = dt.reshape(bsz, seq, 128)
    z3 = z.reshape(bsz, seq, SSM_INNER)
    pad128 = lambda t: jnp.pad(t.reshape(1, SSM_HEADS), ((0, 0), (0, 128 - SSM_HEADS)))
    x_spec = pl.BlockSpec((None, tc, GL), lambda b, t, g: (b, t, g))
    b_spec = pl.BlockSpec((None, tc, SSM_STATE), lambda b, t, g: (b, t, SSM_INNER // SSM_STATE + g))
    c_spec = pl.BlockSpec((None, tc, SSM_STATE), lambda b, t, g: (b, t, (SSM_INNER + SSM_BC) // SSM_STATE + g))
    dt_spec = pl.BlockSpec((None, tc, 128), lambda b, t, g: (b, t, 0))
    p128 = pl.BlockSpec((1, 128), lambda b, t, g: (0, 0))
    out, ssq = pl.pallas_call(
        functools.partial(_ssd_kernel, tc=tc),
        grid=(bsz, seq // tc, G),
        in_specs=[x_spec, b_spec, c_spec, dt_spec, x_spec, p128, p128, p128,
                  pl.BlockSpec((1, GL), lambda b, t, g: (0, g))],
        out_specs=[x_spec, dt_spec],
        out_shape=[jax.ShapeDtypeStruct((bsz, seq, SSM_INNER), BF16),
                   jax.ShapeDtypeStruct((bsz, seq, 128), F32)],
        scratch_shapes=[pltpu.VMEM((G, SSM_STATE, GL), F32)],
        compiler_params=_cparams(("parallel", "arbitrary", "arbitrary")),
        name="ssd_scan",
    )(xbc3, xbc3, xbc3, dt3, z3, pad128(dt_bias), pad128(a_log), pad128(d_skip),
      norm_w.reshape(1, SSM_INNER))
    return out.reshape(bsz * seq, SSM_INNER), ssq.reshape(bsz * seq, 128)


def _pad_cols(w, n):
    return jnp.pad(w, ((0, 0), (0, n - w.shape[1])))


def _pad_rows(w, n):
    return jnp.pad(w, ((0, n - w.shape[0]), (0, 0)))


def kernel(x, c, ada_w, ada_b, ada_table, norm_mix_pre, norm_mix_post, norm_ffn_pre, norm_ffn_post, rwkv_mu, rwkv_w_rkv, rwkv_w0, rwkv_w1, rwkv_w2, rwkv_a0, rwkv_a1, rwkv_a2, rwkv_v0, rwkv_v1, rwkv_v2, rwkv_g1, rwkv_g2, rwkv_k_k, rwkv_k_a, rwkv_r_k, rwkv_ln_w, rwkv_ln_b, rwkv_w_o, ssm_w_in, ssm_conv_w, ssm_conv_b, ssm_dt_bias, ssm_a_log, ssm_d, ssm_norm, ssm_w_out, ffn_w_in, ffn_conv_w, ffn_conv_b, ffn_w_out):
    bsz, seq, d = x.shape
    depth = ada_table.shape[0]
    n = bsz * seq
    bf = lambda t: t.astype(BF16)

    sc_in = jnp.pad(bf(jax.nn.silu(c)), ((0, 16 - bsz), (0, 0)))
    mod = _mm(sc_in, bf(ada_w), bias=ada_b)[:bsz]

    xf = x.reshape(n, d)
    v_first = None
    for layer in range(depth):
        sh_m, sc_m, g_m, sh_f, sc_f, g_f = jnp.split(mod + ada_table[layer], N_MOD, axis=-1)
        idx = layer // 2
        if layer % 2 == 0:
            mu = rwkv_mu[idx]
            xr, xk, xv, xw, xa, xg = _prenorm(xf, norm_mix_pre[layer], sc_m, sh_m, mu, seq)
            r = _mm(xr, bf(rwkv_w_rkv[idx, 0]))
            k = _mm(xk, bf(rwkv_w_rkv[idx, 1]))
            v = _mm(xv, bf(rwkv_w_rkv[idx, 2]))
            wl = _mm(_mm(xw, bf(_pad_cols(rwkv_w1[idx], 128)), act="tanh", out_dtype=BF16),
                     bf(_pad_rows(rwkv_w2[idx], 128)))
            al = _mm(_mm(xa, bf(_pad_cols(rwkv_a1[idx], 128)), out_dtype=BF16),
                     bf(_pad_rows(rwkv_a2[idx], 128)))
            gg = _mm(_mm(xg, bf(rwkv_g1[idx]), act="sigmoid", out_dtype=BF16), bf(rwkv_g2[idx]))
            if idx == 0:
                vmix = None
                v_first = v
            else:
                vl = _mm(_mm(xv, bf(_pad_cols(rwkv_v1[idx - 1], 128)), out_dtype=BF16),
                         bf(_pad_rows(rwkv_v2[idx - 1], 128)))
                vmix = (v_first, vl, rwkv_v0[idx - 1])
            params = (rwkv_w0[idx], rwkv_a0[idx], rwkv_k_k[idx], rwkv_k_a[idx],
                      rwkv_r_k[idx].reshape(d), rwkv_ln_w[idx], rwkv_ln_b[idx])
            yg = _rwkv_scan(r, k, v, wl, al, gg, vmix, params, bsz, seq)
            xf = _mm_post(yg, bf(rwkv_w_o[idx]), xf, g_m, norm_mix_post[layer], seq)
        else:
            (h,) = _prenorm(xf, norm_mix_pre[layer], sc_m, sh_m, None, seq)
            w_in = ssm_w_in[idx]
            z = _mm(h, bf(w_in[:, :SSM_INNER]))
            xbc = _mm_conv(h, [bf(w_in[:, SSM_INNER:SSM_INNER + SSM_CONV_DIM])], [ssm_conv_w[idx]],
                           [ssm_conv_b[idx]], seq, out_dtype=F32)
            dt = _mm(h, bf(_pad_cols(w_in[:, SSM_INNER + SSM_CONV_DIM:], 128)))
            yz, ssq = _ssd_scan(xbc, dt, z, ssm_dt_bias[idx], ssm_a_log[idx], ssm_d[idx], ssm_norm[idx],
                                bsz, seq)
            xf = _mm_post(yz, bf(ssm_w_out[idx]), xf, g_m, norm_mix_post[layer], seq,
                          ssq=ssq, ssq_dim=SSM_INNER)
        (h,) = _prenorm(xf, norm_ffn_pre[layer], sc_f, sh_f, None, seq)
        w_in = ffn_w_in[layer]
        cw = ffn_conv_w[layer]
        cb = ffn_conv_b[layer]
        F = FFN_HIDDEN
        act = _mm_conv(h,
                       [bf(_pad_cols(w_in[:, :F], FFN_PAD)), bf(_pad_cols(w_in[:, F:], FFN_PAD))],
                       [_pad_cols(cw[:, :F], FFN_PAD), _pad_cols(cw[:, F:], FFN_PAD)],
                       [jnp.pad(cb[:F], (0, FFN_PAD - F)), jnp.pad(cb[F:], (0, FFN_PAD - F))],
                       seq, out_dtype=BF16)
        xf = _mm_post(act, bf(_pad_rows(ffn_w_out[layer], FFN_PAD)), xf, g_f, norm_ffn_post[layer], seq)
    return xf.reshape(bsz, seq, d)
```

```python
import functools

import jax
import jax.numpy as jnp
from jax import lax
from jax.experimental import pallas as pl
from jax.experimental.pallas import tpu as pltpu

F32 = jnp.float32
BF16 = jnp.bfloat16

D_MODEL = 2048
N_MOD = 6
RMS_EPS = 1e-6
GN_EPS = 64e-5
HEAD = 64
CHUNK = 64
RWKV_LANES = 256
RWKV_HB = RWKV_LANES // HEAD
RWKV_GROUPS_PER_STEP = 2
SSM_INNER = 2 * D_MODEL
SSM_GROUPS = 8
SSM_HPG = 8
SSM_STATE = 128
SSM_GL = SSM_HPG * HEAD
SSM_BC = SSM_GROUPS * SSM_STATE
SSM_CONV_DIM = SSM_INNER + 2 * SSM_BC
SSM_HEADS = SSM_GROUPS * SSM_HPG
FFN_HIDDEN = 5504
FFN_PAD = 5632
HALO = 16
VMEM_LIMIT = 56 * 1024 * 1024

NT_DIMS = (((1,), (1,)), ((), ()))
TN_DIMS = (((0,), (0,)), ((), ()))


def _cparams(sem):
    return pltpu.CompilerParams(dimension_semantics=sem, vmem_limit_bytes=VMEM_LIMIT)


def _bdot(a, b):
    return jnp.dot(a, b, preferred_element_type=F32)


def _split3(x):
    hi = x.astype(BF16)
    r1 = x - hi.astype(F32)
    mid = r1.astype(BF16)
    lo = (r1 - mid.astype(F32)).astype(BF16)
    return hi, mid, lo


def _sel_dot_l(m01, x):
    return sum(_bdot(m01, p) for p in _split3(x))


def _sel_dot_r(x, m01):
    return sum(_bdot(p, m01) for p in _split3(x))


def _softplus(z):
    return jnp.maximum(z, 0.0) + jnp.log1p(jnp.exp(-jnp.abs(z)))


def _silu(z):
    return z * jax.nn.sigmoid(z)


def _chunk_masks(tc):
    rt = lax.broadcasted_iota(jnp.int32, (tc, tc), 0)
    ct = lax.broadcasted_iota(jnp.int32, (tc, tc), 1)
    same = (rt // CHUNK) == (ct // CHUNK)
    ones = jnp.where(same, 1.0, 0.0).astype(BF16)
    tri = jnp.where(same & (ct <= rt), 1.0, 0.0).astype(BF16)
    return ones, tri


def _prenorm_kernel(x_ref, xh_ref, g_ref, sc_ref, sh_ref, *rest, n_mix, blocks_per_seq):
    def norm(xv):
        ms = jnp.mean(xv * xv, axis=-1, keepdims=True)
        return xv * lax.rsqrt(ms + RMS_EPS) * g_ref[...] * (1.0 + sc_ref[...]) + sh_ref[...]

    h = norm(x_ref[...])
    if n_mix == 0:
        rest[0][...] = h.astype(rest[0].dtype)
        return
    mu_ref, o_refs = rest[0], rest[1:]
    prev = norm(xh_ref[...])[HALO - 1:HALO, :]
    first = (pl.program_id(0) % blocks_per_seq) == 0
    prev = jnp.where(first, 0.0, prev)
    row = lax.broadcasted_iota(jnp.int32, h.shape, 0)
    hprev = jnp.where(row == 0, prev, pltpu.roll(h, 1, axis=0))
    xx = hprev - h
    for m in range(n_mix):
        o_refs[m][...] = (h + xx * mu_ref[m:m + 1, :]).astype(o_refs[m].dtype)


def _prenorm(x, g, sc, sh, mu, seq, tm=256):
    n, d = x.shape
    nb = seq // tm
    n_mix = 0 if mu is None else mu.shape[0]
    row = lambda i: (i, 0)
    halo = lambda i: (jnp.maximum(i * (tm // HALO) - 1, 0), 0)
    per_b = lambda i: (i // nb, 0, 0)
    in_specs = [pl.BlockSpec((tm, d), row), pl.BlockSpec((HALO, d), halo),
                pl.BlockSpec((1, d), lambda i: (0, 0)),
                pl.BlockSpec((None, 1, d), per_b), pl.BlockSpec((None, 1, d), per_b)]
    args = [x, x, g.reshape(1, d), sc[:, None, :], sh[:, None, :]]
    if n_mix:
        in_specs.append(pl.BlockSpec((n_mix, d), lambda i: (0, 0)))
        args.append(mu)
    n_out = max(n_mix, 1)
    out = pl.pallas_call(
        functools.partial(_prenorm_kernel, n_mix=n_mix, blocks_per_seq=nb),
        grid=(n // tm,),
        in_specs=in_specs,
        out_specs=[pl.BlockSpec((tm, d), row)] * n_out,
        out_shape=[jax.ShapeDtypeStruct((n, d), BF16)] * n_out,
        compiler_params=_cparams(("parallel",)),
        name="prenorm",
    )(*args)
    return out


def _mm_kernel(x_ref, w_ref, *rest, act, has_bias):
    o_ref = rest[-1]
    acc = jnp.dot(x_ref[...], w_ref[...], preferred_element_type=F32)
    if has_bias:
        acc = acc + rest[0][...]
    if act == "tanh":
        acc = jnp.tanh(acc)
    elif act == "sigmoid":
        acc = jax.nn.sigmoid(acc)
    o_ref[...] = acc.astype(o_ref.dtype)


def _mm(x, w, *, act=None, bias=None, out_dtype=F32, tm=1024, tn=1024):
    m, k = x.shape
    n = w.shape[1]
    tm = min(tm, m)
    tn = min(tn, n)
    in_specs = [pl.BlockSpec((tm, k), lambda i, j: (i, 0)), pl.BlockSpec((k, tn), lambda i, j: (0, j))]
    args = [x, w]
    if bias is not None:
        in_specs.append(pl.BlockSpec((1, tn), lambda i, j: (0, j)))
        args.append(bias.reshape(1, n))
    return pl.pallas_call(
        functools.partial(_mm_kernel, act=act, has_bias=bias is not None),
        grid=(m // tm, n // tn),
        in_specs=in_specs,
        out_specs=pl.BlockSpec((tm, tn), lambda i, j: (i, j)),
        out_shape=jax.ShapeDtypeStruct((m, n), out_dtype),
        compiler_params=_cparams(("parallel", "parallel")),
        name="mm",
    )(*args)


def _mm_conv_kernel(x_ref, xh_ref, *rest, taps, n_branch, blocks_per_seq, tm, nh):
    w_refs = rest[:n_branch]
    cw_refs = rest[n_branch:2 * n_branch]
    cb_refs = rest[2 * n_branch:3 * n_branch]
    o_ref = rest[3 * n_branch]
    u_refs = rest[3 * n_branch + 1:]
    first = (pl.program_id(0) % blocks_per_seq) == 0
    xh = xh_ref[...]
    xh = jnp.where(first, jnp.zeros_like(xh), xh)
    x = x_ref[...]
    hw = o_ref.shape[1] // nh

    def matmul(h):
        cols = slice(h * hw, (h + 1) * hw)
        for b in range(n_branch):
            u = u_refs[h * n_branch + b]
            u[0:HALO, :] = jnp.dot(xh, w_refs[b][:, cols], preferred_element_type=F32)
            u[HALO:, :] = jnp.dot(x, w_refs[b][:, cols], preferred_element_type=F32)

    def conv(b, h):
        cols = slice(h * hw, (h + 1) * hw)
        u = u_refs[h * n_branch + b]
        acc = cb_refs[b][:, cols]
        for j in range(taps):
            acc = acc + cw_refs[b][j:j + 1, cols] * u[pl.ds(HALO - (taps - 1) + j, tm), :]
        return acc

    def epilogue(h):
        y = _silu(conv(0, h))
        if n_branch == 2:
            y = y * conv(1, h)
        o_ref[:, h * hw:(h + 1) * hw] = y.astype(o_ref.dtype)

    matmul(0)
    for h in range(nh):
        if h + 1 < nh:
            matmul(h + 1)
        epilogue(h)


def _mm_conv(x, ws, cws, cbs, seq, *, out_dtype, tm=1024, tn=512, nh=1):
    m, k = x.shape
    n = ws[0].shape[1]
    taps = cws[0].shape[0]
    tm = min(tm, seq)
    nb = seq // tm
    nbr = len(ws)
    col = lambda i, j: (0, j)
    in_specs = ([pl.BlockSpec((tm, k), lambda i, j: (i, 0)),
                 pl.BlockSpec((HALO, k), lambda i, j: (jnp.maximum(i * (tm // HALO) - 1, 0), 0))]
                + [pl.BlockSpec((k, tn), col)] * nbr
                + [pl.BlockSpec((taps, tn), col)] * nbr
                + [pl.BlockSpec((1, tn), col)] * nbr)
    return pl.pallas_call(
        functools.partial(_mm_conv_kernel, taps=taps, n_branch=nbr, blocks_per_seq=nb, tm=tm, nh=nh),
        grid=(m // tm, n // tn),
        in_specs=in_specs,
        out_specs=pl.BlockSpec((tm, tn), lambda i, j: (i, j)),
        out_shape=jax.ShapeDtypeStruct((m, n), out_dtype),
        scratch_shapes=[pltpu.VMEM((HALO + tm, tn // nh), F32)] * (nh * nbr),
        compiler_params=_cparams(("parallel", "parallel")),
        name="mm_conv",
    )(x, x, *ws, *cws, *[b.reshape(1, n) for b in cbs])


def _mm_post_kernel(a_ref, w_ref, res_ref, gate_ref, g_ref, *rest, nk, ssq_dim):
    if ssq_dim:
        ssq_ref, o_ref, acc_ref = rest
    else:
        o_ref, acc_ref = rest
    kk = pl.program_id(1)

    @pl.when(kk == 0)
    def _():
        acc_ref[...] = jnp.zeros_like(acc_ref)

    acc_ref[...] += jnp.dot(a_ref[...], w_ref[...], preferred_element_type=F32)

    @pl.when(kk == nk - 1)
    def _():
        f = acc_ref[...]
        if ssq_dim:
            f = f * lax.rsqrt(ssq_ref[:, 0:1] * (1.0 / ssq_dim) + RMS_EPS)
        ms = jnp.mean(f * f, axis=-1, keepdims=True)
        y = f * lax.rsqrt(ms + RMS_EPS) * g_ref[...]
        o_ref[...] = res_ref[...] + gate_ref[...] * y


def _mm_post(a, w, res, gate, g, seq, *, ssq=None, ssq_dim=0, tm=1024, tk=512):
    m, k = a.shape
    d = w.shape[1]
    tm = min(tm, seq)
    nk = k // tk
    nb = seq // tm
    row = lambda i, kk: (i, 0)
    in_specs = [pl.BlockSpec((tm, tk), lambda i, kk: (i, kk)),
                pl.BlockSpec((tk, d), lambda i, kk: (kk, 0)),
                pl.BlockSpec((tm, d), row),
                pl.BlockSpec((None, 1, d), lambda i, kk: (i // nb, 0, 0)),
                pl.BlockSpec((1, d), lambda i, kk: (0, 0))]
    args = [a, w, res, gate[:, None, :], g.reshape(1, d)]
    if ssq is not None:
        in_specs.append(pl.BlockSpec((tm, 128), row))
        args.append(ssq)
    return pl.pallas_call(
        functools.partial(_mm_post_kernel, nk=nk, ssq_dim=ssq_dim),
        grid=(m // tm, nk),
        in_specs=in_specs,
        out_specs=pl.BlockSpec((tm, d), row),
        out_shape=jax.ShapeDtypeStruct((m, d), F32),
        scratch_shapes=[pltpu.VMEM((tm, d), F32)],
        compiler_params=_cparams(("parallel", "arbitrary")),
        name="mm_post",
    )(*args)


def _rwkv_scan_kernel(*refs, tc, ng, has_vmix):
    it = iter(refs)
    r_ref, k_ref, v_ref, wl_ref, al_ref, g_ref = (next(it) for _ in range(6))
    if has_vmix:
        vf_ref, vl_ref, v0_ref = (next(it) for _ in range(3))
    w0_ref, a0_ref, kk_ref, ka_ref, rk_ref, lnw_ref, lnb_ref = (next(it) for _ in range(7))
    o_ref = next(it)
    s_ref = next(it)

    L = RWKV_LANES
    C = CHUNK
    NC = tc // C

    @pl.when(pl.program_id(2) == 0)
    def _():
        s_ref[...] = jnp.zeros_like(s_ref)

    ri = lax.broadcasted_iota(jnp.int32, (L, L), 0)
    ci = lax.broadcasted_iota(jnp.int32, (L, L), 1)
    bd_f = jnp.where((ri // HEAD) == (ci // HEAD), 1.0, 0.0).astype(F32)
    bd_b = bd_f.astype(BF16)
    bd_b2 = jnp.concatenate([bd_b, bd_b], axis=1)
    chunk_ones, chunk_tri = _chunk_masks(tc)
    t_i = lax.broadcasted_iota(jnp.int32, (C, L), 0)
    s_i = lax.broadcasted_iota(jnp.int32, (C, L), 1) % HEAD
    strict = s_i < t_i
    incl = s_i <= t_i
    eye_cat = jnp.where(s_i == t_i, 1.0, 0.0).astype(F32)

    def headsum(xv):
        return _sel_dot_r(xv, bd_b)

    def vstack(xb):
        t = jnp.concatenate([xb] * RWKV_HB, axis=0)
        return t * (bd_b if xb.shape[1] == L else bd_b2)

    pro = []
    for g in range(ng):
        ln = slice(g * L, (g + 1) * L)
        r = r_ref[:, ln]
        k0 = k_ref[:, ln]
        v = v_ref[:, ln]
        if has_vmix:
            v = v + (vf_ref[:, ln] - v) * jax.nn.sigmoid(v0_ref[:, ln] + vl_ref[:, ln])
        w = -_softplus(-(w0_ref[:, ln] + wl_ref[:, ln])) - 0.5
        a = jax.nn.sigmoid(a0_ref[:, ln] + al_ref[:, ln])
        kk = k0 * kk_ref[:, ln]
        kk = kk * lax.rsqrt(jnp.maximum(headsum(kk * kk), 1e-24))
        k = k0 * (1.0 + (a - 1.0) * ka_ref[:, ln])
        lw = -jnp.exp(w)
        a_ = -kk
        b_ = kk * a
        cum = _sel_dot_l(chunk_tri, lw)
        cend = _sel_dot_l(chunk_ones, lw)
        e_neg = jnp.exp(-cum)
        e_end = jnp.exp(cend - cum)
        Rt = r * jnp.exp(cum)
        pro.append(dict(
            r=r, k=k, v=v, Rt=Rt,
            At=(a_ * jnp.exp(cum - lw)).astype(BF16), Rtb=Rt.astype(BF16),
            Kt=(k * e_neg).astype(BF16), Bt=(b_ * e_neg).astype(BF16),
            Kh=(k * e_end).astype(BF16), Bh=(b_ * e_end).astype(BF16),
            vb=v.astype(BF16), g_end=jnp.exp(cend)))

    chains = [(g, c) for g in range(ng) for c in range(NC)]
    n = range(len(chains))
    sl = [slice(c * C, (c + 1) * C) for _, c in chains]
    P_ = [pro[g] for g, _ in chains]
    G = [lax.dot_general(jnp.concatenate([P_[i]["At"][sl[i]], P_[i]["Rtb"][sl[i]]], axis=0),
                         jnp.concatenate([vstack(P_[i]["Bt"][sl[i]]), vstack(P_[i]["Kt"][sl[i]])], axis=0),
                         NT_DIMS, preferred_element_type=F32) for i in n]
    A_ab = [jnp.where(strict, G[i][:C, :L], 0.0) for i in n]
    A_akrk = [jnp.concatenate([jnp.where(strict, G[i][:C, L:], 0.0),
                               jnp.where(incl, G[i][C:, L:], 0.0)], axis=0).astype(BF16) for i in n]
    A_rb = [jnp.where(incl, G[i][C:, :L], 0.0).astype(BF16) for i in n]
    AV = [_bdot(A_akrk[i], vstack(P_[i]["vb"][sl[i]])) for i in n]
    T = [eye_cat + A_ab[i] for i in n]
    Ab = [A_ab[i].astype(BF16) for i in n]
    P = [_bdot(Ab[i], vstack(Ab[i])) for i in n]
    for _ in range(4):
        Pb = [P[i].astype(BF16) for i in n]
        res = [_bdot(Pb[i], vstack(jnp.concatenate([T[i].astype(BF16), Pb[i]], axis=1))) for i in n]
        T = [T[i] + res[i][:, :L] for i in n]
        P = [res[i][:, L:] for i in n]
    T = [T[i] + _bdot(P[i].astype(BF16), vstack(T[i].astype(BF16))) for i in n]
    WUb = [_bdot(T[i].astype(BF16),
                 vstack(jnp.concatenate([P_[i]["At"][sl[i]], AV[i][:C].astype(BF16)], axis=1))).astype(BF16)
           for i in n]
    RY = [_bdot(A_rb[i], vstack(WUb[i])) for i in n]
    rhat = [(P_[i]["Rt"][sl[i]] + RY[i][:, :L]).astype(BF16) for i in n]
    yhat = [AV[i][C:] + RY[i][:, L:] for i in n]
    m2t = [(lax.dot_general(WUb[i][:, :L], P_[i]["Bh"][sl[i]], TN_DIMS, preferred_element_type=F32)
            * bd_f).astype(BF16) for i in n]
    dst = [lax.dot_general(jnp.concatenate([P_[i]["vb"][sl[i]], WUb[i][:, L:]], axis=0),
                           jnp.concatenate([P_[i]["Kh"][sl[i]], P_[i]["Bh"][sl[i]]], axis=0),
                           TN_DIMS, preferred_element_type=F32) * bd_f for i in n]

    S = [s_ref[g] for g in range(ng)]
    ys = [[None] * NC for _ in range(ng)]
    for c in range(NC):
        for g in range(ng):
            i = g * NC + c
            Sb = S[g].astype(BF16)
            ys[g][c] = lax.dot_general(rhat[i], Sb, NT_DIMS, preferred_element_type=F32) + yhat[i]
            S[g] = S[g] * pro[g]["g_end"][c * C:c * C + 1, :] + _bdot(Sb, m2t[i]) + dst[i]
    for g in range(ng):
        s_ref[g] = S[g]

    for g in range(ng):
        ln = slice(g * L, (g + 1) * L)
        y = jnp.concatenate(ys[g], axis=0)
        mean = headsum(y) * (1.0 / HEAD)
        yc = y - mean
        var = headsum(yc * yc) * (1.0 / HEAD)
        yn = yc * lax.rsqrt(var + GN_EPS) * lnw_ref[:, ln] + lnb_ref[:, ln]
        yn = yn + headsum(pro[g]["r"] * pro[g]["k"] * rk_ref[:, ln]) * pro[g]["v"]
        o_ref[:, ln] = (yn * g_ref[:, ln]).astype(o_ref.dtype)


def _rwkv_scan(r, k, v, wl, al, g, vmix, params, bsz, seq, tc=256):
    d = D_MODEL
    ng = RWKV_GROUPS_PER_STEP
    W = ng * RWKV_LANES
    to3 = lambda t: t.reshape(bsz, seq, d)
    blk = pl.BlockSpec((None, tc, W), lambda b, h, t: (b, t, h))
    pblk = pl.BlockSpec((1, W), lambda b, h, t: (0, h))
    args = [to3(t) for t in (r, k, v, wl, al, g)]
    in_specs = [blk] * 6
    if vmix is not None:
        args += [to3(vmix[0]), to3(vmix[1]), vmix[2].reshape(1, d)]
        in_specs += [blk, blk, pblk]
    args += [p.reshape(1, d) for p in params]
    in_specs += [pblk] * 7
    out = pl.pallas_call(
        functools.partial(_rwkv_scan_kernel, tc=tc, ng=ng, has_vmix=vmix is not None),
        grid=(bsz, d // W, seq // tc),
        in_specs=in_specs,
        out_specs=blk,
        out_shape=jax.ShapeDtypeStruct((bsz, seq, d), BF16),
        scratch_shapes=[pltpu.VMEM((ng, RWKV_LANES, RWKV_LANES), F32)],
        compiler_params=_cparams(("parallel", "parallel", "arbitrary")),
        name="rwkv_scan",
    )(*args)
    return out.reshape(bsz * seq, d)


def _ssd_kernel(x_ref, b_ref, c_ref, dt_ref, z_ref, dtb_ref, alog_ref, dsk_ref, nw_ref,
                o_ref, ssq_ref, s_ref, *, tc):
    C = CHUNK
    GL = SSM_GL
    NC = tc // C
    g = pl.program_id(2)

    @pl.when((pl.program_id(1) == 0))
    def _():
        s_ref[g] = jnp.zeros((SSM_STATE, GL), F32)

    ej = lax.broadcasted_iota(jnp.int32, (128, GL), 0)
    ec = lax.broadcasted_iota(jnp.int32, (128, GL), 1) // HEAD
    expand = jnp.where(ej == g * SSM_HPG + ec, 1.0, 0.0).astype(BF16)
    _, chunk_tri = _chunk_masks(tc)
    ones_cc = jnp.ones((C, C), BF16)
    t_i = lax.broadcasted_iota(jnp.int32, (C, GL), 0)
    s_i = lax.broadcasted_iota(jnp.int32, (C, GL), 1) % HEAD
    incl = s_i <= t_i
    eye_cat = jnp.where(s_i == t_i, 1.0, 0.0).astype(F32)
    ri = lax.broadcasted_iota(jnp.int32, (GL, GL), 0) // HEAD
    ci = lax.broadcasted_iota(jnp.int32, (GL, GL), 1) // HEAD
    bd_b = jnp.where(ri == ci, 1.0, 0.0).astype(BF16)

    dt = _softplus(dt_ref[...] + dtb_ref[...])
    a_neg = -jnp.exp(alog_ref[...])
    acs_n = _sel_dot_l(chunk_tri, dt * a_neg)
    dt_x = _sel_dot_r(dt, expand)
    acs = _sel_dot_r(acs_n, expand)
    dsk_x = _sel_dot_r(jnp.broadcast_to(dsk_ref[...], (8, 128)), expand)[0:1, :]

    x = x_ref[...].astype(F32)
    xdt = x * dt_x
    xdt_b = xdt.astype(BF16)
    e_acs = jnp.exp(acs)
    b_b = b_ref[...].astype(BF16)
    c_b = c_ref[...].astype(BF16)

    cs = range(NC)
    sl = [slice(c * C, (c + 1) * C) for c in cs]
    acs_end = [acs[c * C + C - 1:c * C + C, :] for c in cs]
    zrow = [_sel_dot_l(ones_cc, acs[sl[c]] * eye_cat) for c in cs]
    cb = [lax.dot_general(c_b[sl[c]], jnp.concatenate([b_b[sl[c]]] * SSM_HPG, axis=0), NT_DIMS,
                          preferred_element_type=F32) for c in cs]
    upd = [lax.dot_general(b_b[sl[c]], (xdt[sl[c]] * jnp.exp(acs_end[c] - acs[sl[c]])).astype(BF16),
                           TN_DIMS, preferred_element_type=F32) for c in cs]
    m = [(cb[c] * jnp.where(incl, jnp.exp(jnp.minimum(acs[sl[c]] - zrow[c], 0.0)), 0.0)).astype(BF16)
         for c in cs]
    y_in = [_bdot(m[c], jnp.concatenate([xdt_b[sl[c]]] * SSM_HPG, axis=0) * bd_b) for c in cs]
    S = s_ref[g]
    S_in = []
    for c in cs:
        S_in.append(S.astype(BF16))
        S = S * jnp.exp(acs_end[c]) + upd[c]
    s_ref[g] = S
    y_st = [_bdot(c_b[sl[c]], S_in[c]) for c in cs]
    y = jnp.concatenate([y_in[c] + y_st[c] * e_acs[sl[c]] for c in cs], axis=0) + x * dsk_x

    yz = y * _silu(z_ref[...])
    part = jnp.sum(yz * yz, axis=-1, keepdims=True)

    @pl.when(g == 0)
    def _():
        ssq_ref[...] = jnp.zeros_like(ssq_ref)

    ssq_ref[...] += jnp.broadcast_to(part, ssq_ref.shape)
    o_ref[...] = (yz * nw_ref[...]).astype(o_ref.dtype)


def _ssd_scan(xbc, dt, z, dt_bias, a_log, d_skip, norm_w, bsz, seq, tc=256):
    G = SSM_GROUPS
    GL = SSM_GL
    xbc3 = xbc.reshape(bsz, seq, SSM_CONV_DIM)
    dt3 = dt.reshape(bsz, seq, 128)
    z3 = z.reshape(bsz, seq, SSM_INNER)
    pad128 = lambda t: jnp.pad(t.reshape(1, SSM_HEADS), ((0, 0), (0, 128 - SSM_HEADS)))
    x_spec = pl.BlockSpec((None, tc, GL), lambda b, t, g: (b, t, g))
    b_spec = pl.BlockSpec((None, tc, SSM_STATE), lambda b, t, g: (b, t, SSM_INNER // SSM_STATE + g))
    c_spec = pl.BlockSpec((None, tc, SSM_STATE), lambda b, t, g: (b, t, (SSM_INNER + SSM_BC) // SSM_STATE + g))
    dt_spec = pl.BlockSpec((None, tc, 128), lambda b, t, g: (b, t, 0))
    p128 = pl.BlockSpec((1, 128), lambda b, t, g: (0, 0))
    out, ssq = pl.pallas_call(
        functools.partial(_ssd_kernel, tc=tc),
        grid=(bsz, seq // tc, G),
        in_specs=[x_spec, b_spec, c_spec, dt_spec, x_spec, p128, p128, p128,
                  pl.BlockSpec((1, GL), lambda b, t, g: (0, g))],
        out_specs=[x_spec, dt_spec],
        out_shape=[jax.ShapeDtypeStruct((bsz, seq, SSM_INNER), BF16),
                   jax.ShapeDtypeStruct((bsz, seq, 128), F32)],
        scratch_shapes=[pltpu.VMEM((G, SSM_STATE, GL), F32)],
        compiler_params=_cparams(("parallel", "arbitrary", "arbitrary")),
        name="ssd_scan",
    )(xbc3, xbc3, xbc3, dt3, z3, pad128(dt_bias), pad128(a_log), pad128(d_skip),
      norm_w.reshape(1, SSM_INNER))
    return out.reshape(bsz * seq, SSM_INNER), ssq.reshape(bsz * seq, 128)


def _pad_cols(w, n):
    return jnp.pad(w, ((0, 0), (0, n - w.shape[1])))


def _pad_rows(w, n):
    return jnp.pad(w, ((0, n - w.shape[0]), (0, 0)))


def kernel(x, c, ada_w, ada_b, ada_table, norm_mix_pre, norm_mix_post, norm_ffn_pre, norm_ffn_post, rwkv_mu, rwkv_w_rkv, rwkv_w0, rwkv_w1, rwkv_w2, rwkv_a0, rwkv_a1, rwkv_a2, rwkv_v0, rwkv_v1, rwkv_v2, rwkv_g1, rwkv_g2, rwkv_k_k, rwkv_k_a, rwkv_r_k, rwkv_ln_w, rwkv_ln_b, rwkv_w_o, ssm_w_in, ssm_conv_w, ssm_conv_b, ssm_dt_bias, ssm_a_log, ssm_d, ssm_norm, ssm_w_out, ffn_w_in, ffn_conv_w, ffn_conv_b, ffn_w_out):
    bsz, seq, d = x.shape
    depth = ada_table.shape[0]
    n = bsz * seq
    bf = lambda t: t.astype(BF16)

    sc_in = jnp.pad(bf(jax.nn.silu(c)), ((0, 16 - bsz), (0, 0)))
    mod = _mm(sc_in, bf(ada_w), bias=ada_b)[:bsz]

    xf = x.reshape(n, d)
    v_first = None
    for layer in range(depth):
        sh_m, sc_m, g_m, sh_f, sc_f, g_f = jnp.split(mod + ada_table[layer], N_MOD, axis=-1)
        idx = layer // 2
        if layer % 2 == 0:
            mu = rwkv_mu[idx]
            xr, xk, xv, xw, xa, xg = _prenorm(xf, norm_mix_pre[layer], sc_m, sh_m, mu, seq)
            r = _mm(xr, bf(rwkv_w_rkv[idx, 0]))
            k = _mm(xk, bf(rwkv_w_rkv[idx, 1]))
            v = _mm(xv, bf(rwkv_w_rkv[idx, 2]))
            wl = _mm(_mm(xw, bf(_pad_cols(rwkv_w1[idx], 128)), act="tanh", out_dtype=BF16),
                     bf(_pad_rows(rwkv_w2[idx], 128)))
            al = _mm(_mm(xa, bf(_pad_cols(rwkv_a1[idx], 128)), out_dtype=BF16),
                     bf(_pad_rows(rwkv_a2[idx], 128)))
            gg = _mm(_mm(xg, bf(rwkv_g1[idx]), act="sigmoid", out_dtype=BF16), bf(rwkv_g2[idx]))
            if idx == 0:
                vmix = None
                v_first = v
            else:
                vl = _mm(_mm(xv, bf(_pad_cols(rwkv_v1[idx - 1], 128)), out_dtype=BF16),
                         bf(_pad_rows(rwkv_v2[idx - 1], 128)))
                vmix = (v_first, vl, rwkv_v0[idx - 1])
            params = (rwkv_w0[idx], rwkv_a0[idx], rwkv_k_k[idx], rwkv_k_a[idx],
                      rwkv_r_k[idx].reshape(d), rwkv_ln_w[idx], rwkv_ln_b[idx])
            yg = _rwkv_scan(r, k, v, wl, al, gg, vmix, params, bsz, seq)
            xf = _mm_post(yg, bf(rwkv_w_o[idx]), xf, g_m, norm_mix_post[layer], seq)
        else:
            (h,) = _prenorm(xf, norm_mix_pre[layer], sc_m, sh_m, None, seq)
            w_in = ssm_w_in[idx]
            z = _mm(h, bf(w_in[:, :SSM_INNER]))
            xbc = _mm_conv(h, [bf(w_in[:, SSM_INNER:SSM_INNER + SSM_CONV_DIM])], [ssm_conv_w[idx]],
                           [ssm_conv_b[idx]], seq, out_dtype=F32, nh=2)
            dt = _mm(h, bf(_pad_cols(w_in[:, SSM_INNER + SSM_CONV_DIM:], 128)))
            yz, ssq = _ssd_scan(xbc, dt, z, ssm_dt_bias[idx], ssm_a_log[idx], ssm_d[idx], ssm_norm[idx],
                                bsz, seq)
            xf = _mm_post(yz, bf(ssm_w_out[idx]), xf, g_m, norm_mix_post[layer], seq,
                          ssq=ssq, ssq_dim=SSM_INNER)
        (h,) = _prenorm(xf, norm_ffn_pre[layer], sc_f, sh_f, None, seq)
        w_in = ffn_w_in[layer]
        cw = ffn_conv_w[layer]
        cb = ffn_conv_b[layer]
        F = FFN_HIDDEN
        act = _mm_conv(h,
                       [bf(_pad_cols(w_in[:, :F], FFN_PAD)), bf(_pad_cols(w_in[:, F:], FFN_PAD))],
                       [_pad_cols(cw[:, :F], FFN_PAD), _pad_cols(cw[:, F:], FFN_PAD)],
                       [jnp.pad(cb[:F], (0, FFN_PAD - F)), jnp.pad(cb[F:], (0, FFN_PAD - F))],
                       seq, out_dtype=BF16)
        xf = _mm_post(act, bf(_pad_rows(ffn_w_out[layer], FFN_PAD)), xf, g_f, norm_ffn_post[layer], seq)
    return xf.reshape(bsz, seq, d)
```

```python
import functools

import jax
import jax.numpy as jnp
from jax import lax
from jax.experimental import pallas as pl
from jax.experimental.pallas import tpu as pltpu

F32 = jnp.float32
BF16 = jnp.bfloat16

D_MODEL = 2048
N_MOD = 6
RMS_EPS = 1e-6
GN_EPS = 64e-5
HEAD = 64
CHUNK = 64
RWKV_LANES = 256
RWKV_HB = RWKV_LANES // HEAD
RWKV_GROUPS_PER_STEP = 2
SSM_INNER = 2 * D_MODEL
SSM_GROUPS = 8
SSM_HPG = 8
SSM_STATE = 128
SSM_GL = SSM_HPG * HEAD
SSM_BC = SSM_GROUPS * SSM_STATE
SSM_CONV_DIM = SSM_INNER + 2 * SSM_BC
SSM_HEADS = SSM_GROUPS * SSM_HPG
FFN_HIDDEN = 5504
FFN_PAD = 5632
HALO = 16
VMEM_LIMIT = 56 * 1024 * 1024

NT_DIMS = (((1,), (1,)), ((), ()))
TN_DIMS = (((0,), (0,)), ((), ()))


def _cparams(sem):
    return pltpu.CompilerParams(dimension_semantics=sem, vmem_limit_bytes=VMEM_LIMIT)


def _bdot(a, b):
    return jnp.dot(a, b, preferred_element_type=F32)


def _split3(x):
    hi = x.astype(BF16)
    r1 = x - hi.astype(F32)
    mid = r1.astype(BF16)
    lo = (r1 - mid.astype(F32)).astype(BF16)
    return hi, mid, lo


def _sel_dot_l(m01, x):
    return sum(_bdot(m01, p) for p in _split3(x))


def _sel_dot_r(x, m01):
    return sum(_bdot(p, m01) for p in _split3(x))


def _softplus(z):
    return jnp.maximum(z, 0.0) + jnp.log1p(jnp.exp(-jnp.abs(z)))


def _silu(z):
    return z * jax.nn.sigmoid(z)


def _chunk_masks(tc):
    rt = lax.broadcasted_iota(jnp.int32, (tc, tc), 0)
    ct = lax.broadcasted_iota(jnp.int32, (tc, tc), 1)
    same = (rt // CHUNK) == (ct // CHUNK)
    ones = jnp.where(same, 1.0, 0.0).astype(BF16)
    tri = jnp.where(same & (ct <= rt), 1.0, 0.0).astype(BF16)
    return ones, tri


def _prenorm_kernel(x_ref, xh_ref, g_ref, sc_ref, sh_ref, *rest, n_mix, blocks_per_seq):
    def norm(xv):
        ms = jnp.mean(xv * xv, axis=-1, keepdims=True)
        return xv * lax.rsqrt(ms + RMS_EPS) * g_ref[...] * (1.0 + sc_ref[...]) + sh_ref[...]

    h = norm(x_ref[...])
    if n_mix == 0:
        rest[0][...] = h.astype(rest[0].dtype)
        return
    mu_ref, o_refs = rest[0], rest[1:]
    prev = norm(xh_ref[...])[HALO - 1:HALO, :]
    first = (pl.program_id(0) % blocks_per_seq) == 0
    prev = jnp.where(first, 0.0, prev)
    row = lax.broadcasted_iota(jnp.int32, h.shape, 0)
    hprev = jnp.where(row == 0, prev, pltpu.roll(h, 1, axis=0))
    xx = hprev - h
    for m in range(n_mix):
        o_refs[m][...] = (h + xx * mu_ref[m:m + 1, :]).astype(o_refs[m].dtype)


def _prenorm(x, g, sc, sh, mu, seq, tm=256):
    n, d = x.shape
    nb = seq // tm
    n_mix = 0 if mu is None else mu.shape[0]
    row = lambda i: (i, 0)
    halo = lambda i: (jnp.maximum(i * (tm // HALO) - 1, 0), 0)
    per_b = lambda i: (i // nb, 0, 0)
    in_specs = [pl.BlockSpec((tm, d), row), pl.BlockSpec((HALO, d), halo),
                pl.BlockSpec((1, d), lambda i: (0, 0)),
                pl.BlockSpec((None, 1, d), per_b), pl.BlockSpec((None, 1, d), per_b)]
    args = [x, x, g.reshape(1, d), sc[:, None, :], sh[:, None, :]]
    if n_mix:
        in_specs.append(pl.BlockSpec((n_mix, d), lambda i: (0, 0)))
        args.append(mu)
    n_out = max(n_mix, 1)
    out = pl.pallas_call(
        functools.partial(_prenorm_kernel, n_mix=n_mix, blocks_per_seq=nb),
        grid=(n // tm,),
        in_specs=in_specs,
        out_specs=[pl.BlockSpec((tm, d), row)] * n_out,
        out_shape=[jax.ShapeDtypeStruct((n, d), BF16)] * n_out,
        compiler_params=_cparams(("parallel",)),
        name="prenorm",
    )(*args)
    return out


def _mm_kernel(x_ref, w_ref, *rest, act, has_bias):
    o_ref = rest[-1]
    acc = jnp.dot(x_ref[...], w_ref[...], preferred_element_type=F32)
    if has_bias:
        acc = acc + rest[0][...]
    if act == "tanh":
        acc = jnp.tanh(acc)
    elif act == "sigmoid":
        acc = jax.nn.sigmoid(acc)
    o_ref[...] = acc.astype(o_ref.dtype)


def _mm(x, w, *, act=None, bias=None, out_dtype=F32, tm=1024, tn=1024):
    m, k = x.shape
    n = w.shape[1]
    tm = min(tm, m)
    tn = min(tn, n)
    in_specs = [pl.BlockSpec((tm, k), lambda i, j: (i, 0)), pl.BlockSpec((k, tn), lambda i, j: (0, j))]
    args = [x, w]
    if bias is not None:
        in_specs.append(pl.BlockSpec((1, tn), lambda i, j: (0, j)))
        args.append(bias.reshape(1, n))
    return pl.pallas_call(
        functools.partial(_mm_kernel, act=act, has_bias=bias is not None),
        grid=(m // tm, n // tn),
        in_specs=in_specs,
        out_specs=pl.BlockSpec((tm, tn), lambda i, j: (i, j)),
        out_shape=jax.ShapeDtypeStruct((m, n), out_dtype),
        compiler_params=_cparams(("parallel", "parallel")),
        name="mm",
    )(*args)


def _mm_conv_kernel(x_ref, xh_ref, *rest, taps, n_branch, blocks_per_seq, tm, nh):
    w_refs = rest[:n_branch]
    cw_refs = rest[n_branch:2 * n_branch]
    cb_refs = rest[2 * n_branch:3 * n_branch]
    o_ref = rest[3 * n_branch]
    u_refs = rest[3 * n_branch + 1:]
    first = (pl.program_id(0) % blocks_per_seq) == 0
    xh = xh_ref[...]
    xh = jnp.where(first, jnp.zeros_like(xh), xh)
    x = x_ref[...]
    hw = o_ref.shape[1] // nh

    def matmul(h):
        cols = slice(h * hw, (h + 1) * hw)
        for b in range(n_branch):
            u = u_refs[h * n_branch + b]
            u[0:HALO, :] = jnp.dot(xh, w_refs[b][:, cols], preferred_element_type=F32)
            u[HALO:, :] = jnp.dot(x, w_refs[b][:, cols], preferred_element_type=F32)

    def conv(b, h):
        cols = slice(h * hw, (h + 1) * hw)
        u = u_refs[h * n_branch + b]
        acc = cb_refs[b][:, cols]
        for j in range(taps):
            acc = acc + cw_refs[b][j:j + 1, cols] * u[pl.ds(HALO - (taps - 1) + j, tm), :]
        return acc

    def epilogue(h):
        y = _silu(conv(0, h))
        if n_branch == 2:
            y = y * conv(1, h)
        o_ref[:, h * hw:(h + 1) * hw] = y.astype(o_ref.dtype)

    matmul(0)
    for h in range(nh):
        if h + 1 < nh:
            matmul(h + 1)
        epilogue(h)


def _mm_conv(x, ws, cws, cbs, seq, *, out_dtype, tm=1024, tn=512, nh=1):
    m, k = x.shape
    n = ws[0].shape[1]
    taps = cws[0].shape[0]
    tm = min(tm, seq)
    nb = seq // tm
    nbr = len(ws)
    col = lambda i, j: (0, j)
    in_specs = ([pl.BlockSpec((tm, k), lambda i, j: (i, 0)),
                 pl.BlockSpec((HALO, k), lambda i, j: (jnp.maximum(i * (tm // HALO) - 1, 0), 0))]
                + [pl.BlockSpec((k, tn), col)] * nbr
                + [pl.BlockSpec((taps, tn), col)] * nbr
                + [pl.BlockSpec((1, tn), col)] * nbr)
    return pl.pallas_call(
        functools.partial(_mm_conv_kernel, taps=taps, n_branch=nbr, blocks_per_seq=nb, tm=tm, nh=nh),
        grid=(m // tm, n // tn),
        in_specs=in_specs,
        out_specs=pl.BlockSpec((tm, tn), lambda i, j: (i, j)),
        out_shape=jax.ShapeDtypeStruct((m, n), out_dtype),
        scratch_shapes=[pltpu.VMEM((HALO + tm, tn // nh), F32)] * (nh * nbr),
        compiler_params=_cparams(("parallel", "parallel")),
        name="mm_conv",
    )(x, x, *ws, *cws, *[b.reshape(1, n) for b in cbs])


def _mm_post_kernel(a_ref, w_ref, res_ref, gate_ref, g_ref, *rest, nk, ssq_dim):
    if ssq_dim:
        ssq_ref, o_ref, acc_ref = rest
    else:
        o_ref, acc_ref = rest
    kk = pl.program_id(1)

    @pl.when(kk == 0)
    def _():
        acc_ref[...] = jnp.zeros_like(acc_ref)

    acc_ref[...] += jnp.dot(a_ref[...], w_ref[...], preferred_element_type=F32)

    @pl.when(kk == nk - 1)
    def _():
        f = acc_ref[...]
        if ssq_dim:
            f = f * lax.rsqrt(ssq_ref[:, 0:1] * (1.0 / ssq_dim) + RMS_EPS)
        ms = jnp.mean(f * f, axis=-1, keepdims=True)
        y = f * lax.rsqrt(ms + RMS_EPS) * g_ref[...]
        o_ref[...] = res_ref[...] + gate_ref[...] * y


def _mm_post(a, w, res, gate, g, seq, *, ssq=None, ssq_dim=0, tm=1024, tk=512):
    m, k = a.shape
    d = w.shape[1]
    tm = min(tm, seq)
    nk = k // tk
    nb = seq // tm
    row = lambda i, kk: (i, 0)
    in_specs = [pl.BlockSpec((tm, tk), lambda i, kk: (i, kk)),
                pl.BlockSpec((tk, d), lambda i, kk: (kk, 0)),
                pl.BlockSpec((tm, d), row),
                pl.BlockSpec((None, 1, d), lambda i, kk: (i // nb, 0, 0)),
                pl.BlockSpec((1, d), lambda i, kk: (0, 0))]
    args = [a, w, res, gate[:, None, :], g.reshape(1, d)]
    if ssq is not None:
        in_specs.append(pl.BlockSpec((tm, 128), row))
        args.append(ssq)
    return pl.pallas_call(
        functools.partial(_mm_post_kernel, nk=nk, ssq_dim=ssq_dim),
        grid=(m // tm, nk),
        in_specs=in_specs,
        out_specs=pl.BlockSpec((tm, d), row),
        out_shape=jax.ShapeDtypeStruct((m, d), F32),
        scratch_shapes=[pltpu.VMEM((tm, d), F32)],
        compiler_params=_cparams(("parallel", "arbitrary")),
        name="mm_post",
    )(*args)


def _rwkv_scan_kernel(*refs, tc, ng, has_vmix):
    it = iter(refs)
    r_ref, k_ref, v_ref, wl_ref, al_ref, g_ref = (next(it) for _ in range(6))
    if has_vmix:
        vf_ref, vl_ref, v0_ref = (next(it) for _ in range(3))
    w0_ref, a0_ref, kk_ref, ka_ref, rk_ref, lnw_ref, lnb_ref = (next(it) for _ in range(7))
    o_ref = next(it)
    s_ref = next(it)

    L = RWKV_LANES
    C = CHUNK
    NC = tc // C

    @pl.when(pl.program_id(2) == 0)
    def _():
        s_ref[...] = jnp.zeros_like(s_ref)

    ri = lax.broadcasted_iota(jnp.int32, (L, L), 0)
    ci = lax.broadcasted_iota(jnp.int32, (L, L), 1)
    bd_f = jnp.where((ri // HEAD) == (ci // HEAD), 1.0, 0.0).astype(F32)
    bd_b = bd_f.astype(BF16)
    bd_b2 = jnp.concatenate([bd_b, bd_b], axis=1)
    chunk_ones, chunk_tri = _chunk_masks(tc)
    t_i = lax.broadcasted_iota(jnp.int32, (C, L), 0)
    s_i = lax.broadcasted_iota(jnp.int32, (C, L), 1) % HEAD
    strict = s_i < t_i
    incl = s_i <= t_i
    eye_cat = jnp.where(s_i == t_i, 1.0, 0.0).astype(F32)

    def headsum(xv):
        return _sel_dot_r(xv, bd_b)

    def vstack(xb):
        t = jnp.concatenate([xb] * RWKV_HB, axis=0)
        return t * (bd_b if xb.shape[1] == L else bd_b2)

    pro = []
    for g in range(ng):
        ln = slice(g * L, (g + 1) * L)
        r = r_ref[:, ln]
        k0 = k_ref[:, ln]
        v = v_ref[:, ln]
        if has_vmix:
            v = v + (vf_ref[:, ln] - v) * jax.nn.sigmoid(v0_ref[:, ln] + vl_ref[:, ln])
        w = -_softplus(-(w0_ref[:, ln] + wl_ref[:, ln])) - 0.5
        a = jax.nn.sigmoid(a0_ref[:, ln] + al_ref[:, ln])
        kk = k0 * kk_ref[:, ln]
        kk = kk * lax.rsqrt(jnp.maximum(headsum(kk * kk), 1e-24))
        k = k0 * (1.0 + (a - 1.0) * ka_ref[:, ln])
        lw = -jnp.exp(w)
        a_ = -kk
        b_ = kk * a
        cum = _sel_dot_l(chunk_tri, lw)
        cend = _sel_dot_l(chunk_ones, lw)
        e_neg = jnp.exp(-cum)
        e_end = jnp.exp(cend - cum)
        Rt = r * jnp.exp(cum)
        pro.append(dict(
            r=r, k=k, v=v, Rt=Rt,
            At=(a_ * jnp.exp(cum - lw)).astype(BF16), Rtb=Rt.astype(BF16),
            Kt=(k * e_neg).astype(BF16), Bt=(b_ * e_neg).astype(BF16),
            Kh=(k * e_end).astype(BF16), Bh=(b_ * e_end).astype(BF16),
            vb=v.astype(BF16), g_end=jnp.exp(cend)))

    chains = [(g, c) for g in range(ng) for c in range(NC)]
    n = range(len(chains))
    sl = [slice(c * C, (c + 1) * C) for _, c in chains]
    P_ = [pro[g] for g, _ in chains]
    G = [lax.dot_general(jnp.concatenate([P_[i]["At"][sl[i]], P_[i]["Rtb"][sl[i]]], axis=0),
                         jnp.concatenate([vstack(P_[i]["Bt"][sl[i]]), vstack(P_[i]["Kt"][sl[i]])], axis=0),
                         NT_DIMS, preferred_element_type=F32) for i in n]
    A_ab = [jnp.where(strict, G[i][:C, :L], 0.0) for i in n]
    A_akrk = [jnp.concatenate([jnp.where(strict, G[i][:C, L:], 0.0),
                               jnp.where(incl, G[i][C:, L:], 0.0)], axis=0).astype(BF16) for i in n]
    A_rb = [jnp.where(incl, G[i][C:, :L], 0.0).astype(BF16) for i in n]
    AV = [_bdot(A_akrk[i], vstack(P_[i]["vb"][sl[i]])) for i in n]
    T = [eye_cat + A_ab[i] for i in n]
    Ab = [A_ab[i].astype(BF16) for i in n]
    P = [_bdot(Ab[i], vstack(Ab[i])) for i in n]
    for _ in range(4):
        Pb = [P[i].astype(BF16) for i in n]
        res = [_bdot(Pb[i], vstack(jnp.concatenate([T[i].astype(BF16), Pb[i]], axis=1))) for i in n]
        T = [T[i] + res[i][:, :L] for i in n]
        P = [res[i][:, L:] for i in n]
    T = [T[i] + _bdot(P[i].astype(BF16), vstack(T[i].astype(BF16))) for i in n]
    WUb = [_bdot(T[i].astype(BF16),
                 vstack(jnp.concatenate([P_[i]["At"][sl[i]], AV[i][:C].astype(BF16)], axis=1))).astype(BF16)
           for i in n]
    RY = [_bdot(A_rb[i], vstack(WUb[i])) for i in n]
    rhat = [(P_[i]["Rt"][sl[i]] + RY[i][:, :L]).astype(BF16) for i in n]
    yhat = [AV[i][C:] + RY[i][:, L:] for i in n]
    m2t = [(lax.dot_general(WUb[i][:, :L], P_[i]["Bh"][sl[i]], TN_DIMS, preferred_element_type=F32)
            * bd_f).astype(BF16) for i in n]
    dst = [lax.dot_general(jnp.concatenate([P_[i]["vb"][sl[i]], WUb[i][:, L:]], axis=0),
                           jnp.concatenate([P_[i]["Kh"][sl[i]], P_[i]["Bh"][sl[i]]], axis=0),
                           TN_DIMS, preferred_element_type=F32) * bd_f for i in n]

    S = [s_ref[g] for g in range(ng)]
    ys = [[None] * NC for _ in range(ng)]
    for c in range(NC):
        for g in range(ng):
            i = g * NC + c
            Sb = S[g].astype(BF16)
            ys[g][c] = lax.dot_general(rhat[i], Sb, NT_DIMS, preferred_element_type=F32) + yhat[i]
            S[g] = S[g] * pro[g]["g_end"][c * C:c * C + 1, :] + _bdot(Sb, m2t[i]) + dst[i]
    for g in range(ng):
        s_ref[g] = S[g]

    for g in range(ng):
        ln = slice(g * L, (g + 1) * L)
        y = jnp.concatenate(ys[g], axis=0)
        mean = headsum(y) * (1.0 / HEAD)
        yc = y - mean
        var = headsum(yc * yc) * (1.0 / HEAD)
        yn = yc * lax.rsqrt(var + GN_EPS) * lnw_ref[:, ln] + lnb_ref[:, ln]
        yn = yn + headsum(pro[g]["r"] * pro[g]["k"] * rk_ref[:, ln]) * pro[g]["v"]
        o_ref[:, ln] = (yn * g_ref[:, ln]).astype(o_ref.dtype)


def _rwkv_scan(r, k, v, wl, al, g, vmix, params, bsz, seq, tc=256):
    d = D_MODEL
    ng = RWKV_GROUPS_PER_STEP
    W = ng * RWKV_LANES
    to3 = lambda t: t.reshape(bsz, seq, d)
    blk = pl.BlockSpec((None, tc, W), lambda b, h, t: (b, t, h))
    pblk = pl.BlockSpec((1, W), lambda b, h, t: (0, h))
    args = [to3(t) for t in (r, k, v, wl, al, g)]
    in_specs = [blk] * 6
    if vmix is not None:
        args += [to3(vmix[0]), to3(vmix[1]), vmix[2].reshape(1, d)]
        in_specs += [blk, blk, pblk]
    args += [p.reshape(1, d) for p in params]
    in_specs += [pblk] * 7
    out = pl.pallas_call(
        functools.partial(_rwkv_scan_kernel, tc=tc, ng=ng, has_vmix=vmix is not None),
        grid=(bsz, d // W, seq // tc),
        in_specs=in_specs,
        out_specs=blk,
        out_shape=jax.ShapeDtypeStruct((bsz, seq, d), BF16),
        scratch_shapes=[pltpu.VMEM((ng, RWKV_LANES, RWKV_LANES), F32)],
        compiler_params=_cparams(("parallel", "parallel", "arbitrary")),
        name="rwkv_scan",
    )(*args)
    return out.reshape(bsz * seq, d)


def _ssd_kernel(x_ref, b_ref, c_ref, dt_ref, z_ref, dtb_ref, alog_ref, dsk_ref, nw_ref,
                o_ref, ssq_ref, s_ref, *, tc):
    C = CHUNK
    GL = SSM_GL
    NC = tc // C
    g = pl.program_id(2)

    @pl.when((pl.program_id(1) == 0))
    def _():
        s_ref[g] = jnp.zeros((SSM_STATE, GL), F32)

    ej = lax.broadcasted_iota(jnp.int32, (128, GL), 0)
    ec = lax.broadcasted_iota(jnp.int32, (128, GL), 1) // HEAD
    expand = jnp.where(ej == g * SSM_HPG + ec, 1.0, 0.0).astype(BF16)
    _, chunk_tri = _chunk_masks(tc)
    ones_cc = jnp.ones((C, C), BF16)
    t_i = lax.broadcasted_iota(jnp.int32, (C, GL), 0)
    s_i = lax.broadcasted_iota(jnp.int32, (C, GL), 1) % HEAD
    incl = s_i <= t_i
    eye_cat = jnp.where(s_i == t_i, 1.0, 0.0).astype(F32)
    ri = lax.broadcasted_iota(jnp.int32, (GL, GL), 0) // HEAD
    ci = lax.broadcasted_iota(jnp.int32, (GL, GL), 1) // HEAD
    bd_b = jnp.where(ri == ci, 1.0, 0.0).astype(BF16)

    dt = _softplus(dt_ref[...] + dtb_ref[...])
    a_neg = -jnp.exp(alog_ref[...])
    acs_n = _sel_dot_l(chunk_tri, dt * a_neg)
    dt_x = _sel_dot_r(dt, expand)
    acs = _sel_dot_r(acs_n, expand)
    dsk_x = _sel_dot_r(jnp.broadcast_to(dsk_ref[...], (8, 128)), expand)[0:1, :]

    x = x_ref[...].astype(F32)
    xdt = x * dt_x
    xdt_b = xdt.astype(BF16)
    e_acs = jnp.exp(acs)
    b_b = b_ref[...].astype(BF16)
    c_b = c_ref[...].astype(BF16)

    cs = range(NC)
    sl = [slice(c * C, (c + 1) * C) for c in cs]
    acs_end = [acs[c * C + C - 1:c * C + C, :] for c in cs]
    zrow = [_sel_dot_l(ones_cc, acs[sl[c]] * eye_cat) for c in cs]
    cb = [lax.dot_general(c_b[sl[c]], jnp.concatenate([b_b[sl[c]]] * SSM_HPG, axis=0), NT_DIMS,
                          preferred_element_type=F32) for c in cs]
    upd = [lax.dot_general(b_b[sl[c]], (xdt[sl[c]] * jnp.exp(acs_end[c] - acs[sl[c]])).astype(BF16),
                           TN_DIMS, preferred_element_type=F32) for c in cs]
    m = [(cb[c] * jnp.where(incl, jnp.exp(jnp.minimum(acs[sl[c]] - zrow[c], 0.0)), 0.0)).astype(BF16)
         for c in cs]
    y_in = [_bdot(m[c], jnp.concatenate([xdt_b[sl[c]]] * SSM_HPG, axis=0) * bd_b) for c in cs]
    S = s_ref[g]
    S_in = []
    for c in cs:
        S_in.append(S.astype(BF16))
        S = S * jnp.exp(acs_end[c]) + upd[c]
    s_ref[g] = S
    y_st = [_bdot(c_b[sl[c]], S_in[c]) for c in cs]
    y = jnp.concatenate([y_in[c] + y_st[c] * e_acs[sl[c]] for c in cs], axis=0) + x * dsk_x

    yz = y * _silu(z_ref[...])
    part = jnp.sum(yz * yz, axis=-1, keepdims=True)

    @pl.when(g == 0)
    def _():
        ssq_ref[...] = jnp.zeros_like(ssq_ref)

    ssq_ref[...] += jnp.broadcast_to(part, ssq_ref.shape)
    o_ref[...] = (yz * nw_ref[...]).astype(o_ref.dtype)


def _ssd_scan(xbc, dt, z, dt_bias, a_log, d_skip, norm_w, bsz, seq, tc=512):
    G = SSM_GROUPS
    GL = SSM_GL
    xbc3 = xbc.reshape(bsz, seq, SSM_CONV_DIM)
    dt3 = dt.reshape(bsz, seq, 128)
    z3 = z.reshape(bsz, seq, SSM_INNER)
    pad128 = lambda t: jnp.pad(t.reshape(1, SSM_HEADS), ((0, 0), (0, 128 - SSM_HEADS)))
    x_spec = pl.BlockSpec((None, tc, GL), lambda b, t, g: (b, t, g))
    b_spec = pl.BlockSpec((None, tc, SSM_STATE), lambda b, t, g: (b, t, SSM_INNER // SSM_STATE + g))
    c_spec = pl.BlockSpec((None, tc, SSM_STATE), lambda b, t, g: (b, t, (SSM_INNER + SSM_BC) // SSM_STATE + g))
    dt_spec = pl.BlockSpec((None, tc, 128), lambda b, t, g: (b, t, 0))
    p128 = pl.BlockSpec((1, 128), lambda b, t, g: (0, 0))
    out, ssq = pl.pallas_call(
        functools.partial(_ssd_kernel, tc=tc),
        grid=(bsz, seq // tc, G),
        in_specs=[x_spec, b_spec, c_spec, dt_spec, x_spec, p128, p128, p128,
                  pl.BlockSpec((1, GL), lambda b, t, g: (0, g))],
        out_specs=[x_spec, dt_spec],
        out_shape=[jax.ShapeDtypeStruct((bsz, seq, SSM_INNER), BF16),
                   jax.ShapeDtypeStruct((bsz, seq, 128), F32)],
        scratch_shapes=[pltpu.VMEM((G, SSM_STATE, GL), F32)],
        compiler_params=_cparams(("parallel", "arbitrary", "arbitrary")),
        name="ssd_scan",
    )(xbc3, xbc3, xbc3, dt3, z3, pad128(dt_bias), pad128(a_log), pad128(d_skip),
      norm_w.reshape(1, SSM_INNER))
    return out.reshape(bsz * seq, SSM_INNER), ssq.reshape(bsz * seq, 128)


def _pad_cols(w, n):
    return jnp.pad(w, ((0, 0), (0, n - w.shape[1])))


def _pad_rows(w, n):
    return jnp.pad(w, ((0, n - w.shape[0]), (0, 0)))


def kernel(x, c, ada_w, ada_b, ada_table, norm_mix_pre, norm_mix_post, norm_ffn_pre, norm_ffn_post, rwkv_mu, rwkv_w_rkv, rwkv_w0, rwkv_w1, rwkv_w2, rwkv_a0, rwkv_a1, rwkv_a2, rwkv_v0, rwkv_v1, rwkv_v2, rwkv_g1, rwkv_g2, rwkv_k_k, rwkv_k_a, rwkv_r_k, rwkv_ln_w, rwkv_ln_b, rwkv_w_o, ssm_w_in, ssm_conv_w, ssm_conv_b, ssm_dt_bias, ssm_a_log, ssm_d, ssm_norm, ssm_w_out, ffn_w_in, ffn_conv_w, ffn_conv_b, ffn_w_out):
    bsz, seq, d = x.shape
    depth = ada_table.shape[0]
    n = bsz * seq
    bf = lambda t: t.astype(BF16)

    sc_in = jnp.pad(bf(jax.nn.silu(c)), ((0, 16 - bsz), (0, 0)))
    mod = _mm(sc_in, bf(ada_w), bias=ada_b)[:bsz]

    xf = x.reshape(n, d)
    v_first = None
    for layer in range(depth):
        sh_m, sc_m, g_m, sh_f, sc_f, g_f = jnp.split(mod + ada_table[layer], N_MOD, axis=-1)
        idx = layer // 2
        if layer % 2 == 0:
            mu = rwkv_mu[idx]
            xr, xk, xv, xw, xa, xg = _prenorm(xf, norm_mix_pre[layer], sc_m, sh_m, mu, seq)
            r = _mm(xr, bf(rwkv_w_rkv[idx, 0]))
            k = _mm(xk, bf(rwkv_w_rkv[idx, 1]))
            v = _mm(xv, bf(rwkv_w_rkv[idx, 2]))
            wl = _mm(_mm(xw, bf(_pad_cols(rwkv_w1[idx], 128)), act="tanh", out_dtype=BF16),
                     bf(_pad_rows(rwkv_w2[idx], 128)))
            al = _mm(_mm(xa, bf(_pad_cols(rwkv_a1[idx], 128)), out_dtype=BF16),
                     bf(_pad_rows(rwkv_a2[idx], 128)))
            gg = _mm(_mm(xg, bf(rwkv_g1[idx]), act="sigmoid", out_dtype=BF16), bf(rwkv_g2[idx]))
            if idx == 0:
                vmix = None
                v_first = v
            else:
                vl = _mm(_mm(xv, bf(_pad_cols(rwkv_v1[idx - 1], 128)), out_dtype=BF16),
                         bf(_pad_rows(rwkv_v2[idx - 1], 128)))
                vmix = (v_first, vl, rwkv_v0[idx - 1])
            params = (rwkv_w0[idx], rwkv_a0[idx], rwkv_k_k[idx], rwkv_k_a[idx],
                      rwkv_r_k[idx].reshape(d), rwkv_ln_w[idx], rwkv_ln_b[idx])
            yg = _rwkv_scan(r, k, v, wl, al, gg, vmix, params, bsz, seq)
            xf = _mm_post(yg, bf(rwkv_w_o[idx]), xf, g_m, norm_mix_post[layer], seq)
        else:
            (h,) = _prenorm(xf, norm_mix_pre[layer], sc_m, sh_m, None, seq)
            w_in = ssm_w_in[idx]
            z = _mm(h, bf(w_in[:, :SSM_INNER]))
            xbc = _mm_conv(h, [bf(w_in[:, SSM_INNER:SSM_INNER + SSM_CONV_DIM])], [ssm_conv_w[idx]],
                           [ssm_conv_b[idx]], seq, out_dtype=F32, nh=2)
            dt = _mm(h, bf(_pad_cols(w_in[:, SSM_INNER + SSM_CONV_DIM:], 128)))
            yz, ssq = _ssd_scan(xbc, dt, z, ssm_dt_bias[idx], ssm_a_log[idx], ssm_d[idx], ssm_norm[idx],
                                bsz, seq)
            xf = _mm_post(yz, bf(ssm_w_out[idx]), xf, g_m, norm_mix_post[layer], seq,
                          ssq=ssq, ssq_dim=SSM_INNER)
        (h,) = _prenorm(xf, norm_ffn_pre[layer], sc_f, sh_f, None, seq)
        w_in = ffn_w_in[layer]
        cw = ffn_conv_w[layer]
        cb = ffn_conv_b[layer]
        F = FFN_HIDDEN
        act = _mm_conv(h,
                       [bf(_pad_cols(w_in[:, :F], FFN_PAD)), bf(_pad_cols(w_in[:, F:], FFN_PAD))],
                       [_pad_cols(cw[:, :F], FFN_PAD), _pad_cols(cw[:, F:], FFN_PAD)],
                       [jnp.pad(cb[:F], (0, FFN_PAD - F)), jnp.pad(cb[F:], (0, FFN_PAD - F))],
                       seq, out_dtype=BF16)
        xf = _mm_post(act, bf(_pad_rows(ffn_w_out[layer], FFN_PAD)), xf, g_f, norm_ffn_post[layer], seq)
    return xf.reshape(bsz, seq, d)
```

```python
import functools

import jax
import jax.numpy as jnp
from jax import lax
from jax.experimental import pallas as pl
from jax.experimental.pallas import tpu as pltpu

F32 = jnp.float32
BF16 = jnp.bfloat16

D_MODEL = 2048
N_MOD = 6
RMS_EPS = 1e-6
GN_EPS = 64e-5
HEAD = 64
CHUNK = 64
RWKV_LANES = 256
RWKV_HB = RWKV_LANES // HEAD
RWKV_GROUPS_PER_STEP = 2
SSM_INNER = 2 * D_MODEL
SSM_GROUPS = 8
SSM_HPG = 8
SSM_STATE = 128
SSM_GL = SSM_HPG * HEAD
SSM_BC = SSM_GROUPS * SSM_STATE
SSM_CONV_DIM = SSM_INNER + 2 * SSM_BC
SSM_HEADS = SSM_GROUPS * SSM_HPG
FFN_HIDDEN = 5504
FFN_PAD = 5632
HALO = 16
VMEM_LIMIT = 56 * 1024 * 1024

NT_DIMS = (((1,), (1,)), ((), ()))
TN_DIMS = (((0,), (0,)), ((), ()))


def _cparams(sem):
    return pltpu.CompilerParams(dimension_semantics=sem, vmem_limit_bytes=VMEM_LIMIT)


def _bdot(a, b):
    return jnp.dot(a, b, preferred_element_type=F32)


def _split3(x):
    hi = x.astype(BF16)
    r1 = x - hi.astype(F32)
    mid = r1.astype(BF16)
    lo = (r1 - mid.astype(F32)).astype(BF16)
    return hi, mid, lo


def _sel_dot_l(m01, x):
    return sum(_bdot(m01, p) for p in _split3(x))


def _sel_dot_r(x, m01):
    return sum(_bdot(p, m01) for p in _split3(x))


def _softplus(z):
    return jnp.maximum(z, 0.0) + jnp.log1p(jnp.exp(-jnp.abs(z)))


def _silu(z):
    return z * jax.nn.sigmoid(z)


def _chunk_masks(tc):
    rt = lax.broadcasted_iota(jnp.int32, (tc, tc), 0)
    ct = lax.broadcasted_iota(jnp.int32, (tc, tc), 1)
    same = (rt // CHUNK) == (ct // CHUNK)
    ones = jnp.where(same, 1.0, 0.0).astype(BF16)
    tri = jnp.where(same & (ct <= rt), 1.0, 0.0).astype(BF16)
    return ones, tri


def _prenorm_kernel(x_ref, xh_ref, g_ref, sc_ref, sh_ref, *rest, n_mix, blocks_per_seq):
    def norm(xv):
        ms = jnp.mean(xv * xv, axis=-1, keepdims=True)
        return xv * lax.rsqrt(ms + RMS_EPS) * g_ref[...] * (1.0 + sc_ref[...]) + sh_ref[...]

    h = norm(x_ref[...])
    if n_mix == 0:
        rest[0][...] = h.astype(rest[0].dtype)
        return
    mu_ref, o_refs = rest[0], rest[1:]
    prev = norm(xh_ref[...])[HALO - 1:HALO, :]
    first = (pl.program_id(0) % blocks_per_seq) == 0
    prev = jnp.where(first, 0.0, prev)
    row = lax.broadcasted_iota(jnp.int32, h.shape, 0)
    hprev = jnp.where(row == 0, prev, pltpu.roll(h, 1, axis=0))
    xx = hprev - h
    for m in range(n_mix):
        o_refs[m][...] = (h + xx * mu_ref[m:m + 1, :]).astype(o_refs[m].dtype)


def _prenorm(x, g, sc, sh, mu, seq, tm=256):
    n, d = x.shape
    nb = seq // tm
    n_mix = 0 if mu is None else mu.shape[0]
    row = lambda i: (i, 0)
    halo = lambda i: (jnp.maximum(i * (tm // HALO) - 1, 0), 0)
    per_b = lambda i: (i // nb, 0, 0)
    in_specs = [pl.BlockSpec((tm, d), row), pl.BlockSpec((HALO, d), halo),
                pl.BlockSpec((1, d), lambda i: (0, 0)),
                pl.BlockSpec((None, 1, d), per_b), pl.BlockSpec((None, 1, d), per_b)]
    args = [x, x, g.reshape(1, d), sc[:, None, :], sh[:, None, :]]
    if n_mix:
        in_specs.append(pl.BlockSpec((n_mix, d), lambda i: (0, 0)))
        args.append(mu)
    n_out = max(n_mix, 1)
    out = pl.pallas_call(
        functools.partial(_prenorm_kernel, n_mix=n_mix, blocks_per_seq=nb),
        grid=(n // tm,),
        in_specs=in_specs,
        out_specs=[pl.BlockSpec((tm, d), row)] * n_out,
        out_shape=[jax.ShapeDtypeStruct((n, d), BF16)] * n_out,
        compiler_params=_cparams(("parallel",)),
        name="prenorm",
    )(*args)
    return out


def _mm_kernel(x_ref, w_ref, *rest, act, has_bias):
    o_ref = rest[-1]
    acc = jnp.dot(x_ref[...], w_ref[...], preferred_element_type=F32)
    if has_bias:
        acc = acc + rest[0][...]
    if act == "tanh":
        acc = jnp.tanh(acc)
    elif act == "sigmoid":
        acc = jax.nn.sigmoid(acc)
    o_ref[...] = acc.astype(o_ref.dtype)


def _mm(x, w, *, act=None, bias=None, out_dtype=F32, tm=1024, tn=1024):
    m, k = x.shape
    n = w.shape[1]
    tm = min(tm, m)
    tn = min(tn, n)
    in_specs = [pl.BlockSpec((tm, k), lambda i, j: (i, 0)), pl.BlockSpec((k, tn), lambda i, j: (0, j))]
    args = [x, w]
    if bias is not None:
        in_specs.append(pl.BlockSpec((1, tn), lambda i, j: (0, j)))
        args.append(bias.reshape(1, n))
    return pl.pallas_call(
        functools.partial(_mm_kernel, act=act, has_bias=bias is not None),
        grid=(m // tm, n // tn),
        in_specs=in_specs,
        out_specs=pl.BlockSpec((tm, tn), lambda i, j: (i, j)),
        out_shape=jax.ShapeDtypeStruct((m, n), out_dtype),
        compiler_params=_cparams(("parallel", "parallel")),
        name="mm",
    )(*args)


def _mm_conv_kernel(x_ref, xh_ref, *rest, taps, n_branch, blocks_per_seq, tm, nh):
    w_refs = rest[:n_branch]
    cw_refs = rest[n_branch:2 * n_branch]
    cb_refs = rest[2 * n_branch:3 * n_branch]
    o_ref = rest[3 * n_branch]
    u_refs = rest[3 * n_branch + 1:]
    first = (pl.program_id(0) % blocks_per_seq) == 0
    xh = xh_ref[...]
    xh = jnp.where(first, jnp.zeros_like(xh), xh)
    x = x_ref[...]
    hw = o_ref.shape[1] // nh

    def matmul(h):
        cols = slice(h * hw, (h + 1) * hw)
        for b in range(n_branch):
            u = u_refs[h * n_branch + b]
            u[0:HALO, :] = jnp.dot(xh, w_refs[b][:, cols], preferred_element_type=F32)
            u[HALO:, :] = jnp.dot(x, w_refs[b][:, cols], preferred_element_type=F32)

    def conv(b, h):
        cols = slice(h * hw, (h + 1) * hw)
        u = u_refs[h * n_branch + b]
        acc = cb_refs[b][:, cols]
        for j in range(taps):
            acc = acc + cw_refs[b][j:j + 1, cols] * u[pl.ds(HALO - (taps - 1) + j, tm), :]
        return acc

    def epilogue(h):
        y = _silu(conv(0, h))
        if n_branch == 2:
            y = y * conv(1, h)
        o_ref[:, h * hw:(h + 1) * hw] = y.astype(o_ref.dtype)

    matmul(0)
    for h in range(nh):
        if h + 1 < nh:
            matmul(h + 1)
        epilogue(h)


def _mm_conv(x, ws, cws, cbs, seq, *, out_dtype, tm=1024, tn=512, nh=1):
    m, k = x.shape
    n = ws[0].shape[1]
    taps = cws[0].shape[0]
    tm = min(tm, seq)
    nb = seq // tm
    nbr = len(ws)
    col = lambda i, j: (0, j)
    in_specs = ([pl.BlockSpec((tm, k), lambda i, j: (i, 0)),
                 pl.BlockSpec((HALO, k), lambda i, j: (jnp.maximum(i * (tm // HALO) - 1, 0), 0))]
                + [pl.BlockSpec((k, tn), col)] * nbr
                + [pl.BlockSpec((taps, tn), col)] * nbr
                + [pl.BlockSpec((1, tn), col)] * nbr)
    return pl.pallas_call(
        functools.partial(_mm_conv_kernel, taps=taps, n_branch=nbr, blocks_per_seq=nb, tm=tm, nh=nh),
        grid=(m // tm, n // tn),
        in_specs=in_specs,
        out_specs=pl.BlockSpec((tm, tn), lambda i, j: (i, j)),
        out_shape=jax.ShapeDtypeStruct((m, n), out_dtype),
        scratch_shapes=[pltpu.VMEM((HALO + tm, tn // nh), F32)] * (nh * nbr),
        compiler_params=_cparams(("parallel", "parallel")),
        name="mm_conv",
    )(x, x, *ws, *cws, *[b.reshape(1, n) for b in cbs])


def _mm_post_kernel(a_ref, w_ref, res_ref, gate_ref, g_ref, *rest, nk, ssq_dim):
    if ssq_dim:
        ssq_ref, o_ref, acc_ref = rest
    else:
        o_ref, acc_ref = rest
    kk = pl.program_id(1)

    @pl.when(kk == 0)
    def _():
        acc_ref[...] = jnp.zeros_like(acc_ref)

    acc_ref[...] += jnp.dot(a_ref[...], w_ref[...], preferred_element_type=F32)

    @pl.when(kk == nk - 1)
    def _():
        f = acc_ref[...]
        if ssq_dim:
            f = f * lax.rsqrt(ssq_ref[:, 0:1] * (1.0 / ssq_dim) + RMS_EPS)
        ms = jnp.mean(f * f, axis=-1, keepdims=True)
        y = f * lax.rsqrt(ms + RMS_EPS) * g_ref[...]
        o_ref[...] = res_ref[...] + gate_ref[...] * y


def _mm_post(a, w, res, gate, g, seq, *, ssq=None, ssq_dim=0, tm=1024, tk=512):
    m, k = a.shape
    d = w.shape[1]
    tm = min(tm, seq)
    nk = k // tk
    nb = seq // tm
    row = lambda i, kk: (i, 0)
    in_specs = [pl.BlockSpec((tm, tk), lambda i, kk: (i, kk)),
                pl.BlockSpec((tk, d), lambda i, kk: (kk, 0)),
                pl.BlockSpec((tm, d), row),
                pl.BlockSpec((None, 1, d), lambda i, kk: (i // nb, 0, 0)),
                pl.BlockSpec((1, d), lambda i, kk: (0, 0))]
    args = [a, w, res, gate[:, None, :], g.reshape(1, d)]
    if ssq is not None:
        in_specs.append(pl.BlockSpec((tm, 128), row))
        args.append(ssq)
    return pl.pallas_call(
        functools.partial(_mm_post_kernel, nk=nk, ssq_dim=ssq_dim),
        grid=(m // tm, nk),
        in_specs=in_specs,
        out_specs=pl.BlockSpec((tm, d), row),
        out_shape=jax.ShapeDtypeStruct((m, d), F32),
        scratch_shapes=[pltpu.VMEM((tm, d), F32)],
        compiler_params=_cparams(("parallel", "arbitrary")),
        name="mm_post",
    )(*args)


def _rwkv_scan_kernel(*refs, tc, ng, has_vmix):
    it = iter(refs)
    r_ref, k_ref, v_ref, wl_ref, al_ref, g_ref = (next(it) for _ in range(6))
    if has_vmix:
        vf_ref, vl_ref, v0_ref = (next(it) for _ in range(3))
    w0_ref, a0_ref, kk_ref, ka_ref, rk_ref, lnw_ref, lnb_ref = (next(it) for _ in range(7))
    o_ref = next(it)
    s_ref = next(it)

    L = RWKV_LANES
    C = CHUNK
    NC = tc // C

    @pl.when(pl.program_id(2) == 0)
    def _():
        s_ref[...] = jnp.zeros_like(s_ref)

    ri = lax.broadcasted_iota(jnp.int32, (L, L), 0)
    ci = lax.broadcasted_iota(jnp.int32, (L, L), 1)
    bd_f = jnp.where((ri // HEAD) == (ci // HEAD), 1.0, 0.0).astype(F32)
    bd_b = bd_f.astype(BF16)
    bd_b2 = jnp.concatenate([bd_b, bd_b], axis=1)
    chunk_ones, chunk_tri = _chunk_masks(tc)
    t_i = lax.broadcasted_iota(jnp.int32, (C, L), 0)
    s_i = lax.broadcasted_iota(jnp.int32, (C, L), 1) % HEAD
    strict = s_i < t_i
    incl = s_i <= t_i
    eye_cat = jnp.where(s_i == t_i, 1.0, 0.0).astype(F32)

    def headsum(xv):
        return _sel_dot_r(xv, bd_b)

    def vstack(xb):
        t = jnp.concatenate([xb] * RWKV_HB, axis=0)
        return t * (bd_b if xb.shape[1] == L else bd_b2)

    pro = []
    for g in range(ng):
        ln = slice(g * L, (g + 1) * L)
        r = r_ref[:, ln]
        k0 = k_ref[:, ln]
        v = v_ref[:, ln]
        if has_vmix:
            v = v + (vf_ref[:, ln] - v) * jax.nn.sigmoid(v0_ref[:, ln] + vl_ref[:, ln])
        w = -_softplus(-(w0_ref[:, ln] + wl_ref[:, ln])) - 0.5
        a = jax.nn.sigmoid(a0_ref[:, ln] + al_ref[:, ln])
        kk = k0 * kk_ref[:, ln]
        kk = kk * lax.rsqrt(jnp.maximum(headsum(kk * kk), 1e-24))
        k = k0 * (1.0 + (a - 1.0) * ka_ref[:, ln])
        lw = -jnp.exp(w)
        a_ = -kk
        b_ = kk * a
        cum = _sel_dot_l(chunk_tri, lw)
        cend = _sel_dot_l(chunk_ones, lw)
        e_neg = jnp.exp(-cum)
        e_end = jnp.exp(cend - cum)
        Rt = r * jnp.exp(cum)
        pro.append(dict(
            r=r, k=k, v=v, Rt=Rt,
            At=(a_ * jnp.exp(cum - lw)).astype(BF16), Rtb=Rt.astype(BF16),
            Kt=(k * e_neg).astype(BF16), Bt=(b_ * e_neg).astype(BF16),
            Kh=(k * e_end).astype(BF16), Bh=(b_ * e_end).astype(BF16),
            vb=v.astype(BF16), g_end=jnp.exp(cend)))

    chains = [(g, c) for g in range(ng) for c in range(NC)]
    n = range(len(chains))
    sl = [slice(c * C, (c + 1) * C) for _, c in chains]
    P_ = [pro[g] for g, _ in chains]
    G = [lax.dot_general(jnp.concatenate([P_[i]["At"][sl[i]], P_[i]["Rtb"][sl[i]]], axis=0),
                         jnp.concatenate([vstack(P_[i]["Bt"][sl[i]]), vstack(P_[i]["Kt"][sl[i]])], axis=0),
                         NT_DIMS, preferred_element_type=F32) for i in n]
    A_ab = [jnp.where(strict, G[i][:C, :L], 0.0) for i in n]
    A_akrk = [jnp.concatenate([jnp.where(strict, G[i][:C, L:], 0.0),
                               jnp.where(incl, G[i][C:, L:], 0.0)], axis=0).astype(BF16) for i in n]
    A_rb = [jnp.where(incl, G[i][C:, :L], 0.0).astype(BF16) for i in n]
    AV = [_bdot(A_akrk[i], vstack(P_[i]["vb"][sl[i]])) for i in n]
    T = [eye_cat + A_ab[i] for i in n]
    Ab = [A_ab[i].astype(BF16) for i in n]
    P = [_bdot(Ab[i], vstack(Ab[i])) for i in n]
    for _ in range(4):
        Pb = [P[i].astype(BF16) for i in n]
        res = [_bdot(Pb[i], vstack(jnp.concatenate([T[i].astype(BF16), Pb[i]], axis=1))) for i in n]
        T = [T[i] + res[i][:, :L] for i in n]
        P = [res[i][:, L:] for i in n]
    T = [T[i] + _bdot(P[i].astype(BF16), vstack(T[i].astype(BF16))) for i in n]
    WUb = [_bdot(T[i].astype(BF16),
                 vstack(jnp.concatenate([P_[i]["At"][sl[i]], AV[i][:C].astype(BF16)], axis=1))).astype(BF16)
           for i in n]
    RY = [_bdot(A_rb[i], vstack(WUb[i])) for i in n]
    rhat = [(P_[i]["Rt"][sl[i]] + RY[i][:, :L]).astype(BF16) for i in n]
    yhat = [AV[i][C:] + RY[i][:, L:] for i in n]
    m2t = [(lax.dot_general(WUb[i][:, :L], P_[i]["Bh"][sl[i]], TN_DIMS, preferred_element_type=F32)
            * bd_f).astype(BF16) for i in n]
    dst = [lax.dot_general(jnp.concatenate([P_[i]["vb"][sl[i]], WUb[i][:, L:]], axis=0),
                           jnp.concatenate([P_[i]["Kh"][sl[i]], P_[i]["Bh"][sl[i]]], axis=0),
                           TN_DIMS, preferred_element_type=F32) * bd_f for i in n]

    S = [s_ref[g] for g in range(ng)]
    ys = [[None] * NC for _ in range(ng)]
    for c in range(NC):
        for g in range(ng):
            i = g * NC + c
            Sb = S[g].astype(BF16)
            ys[g][c] = lax.dot_general(rhat[i], Sb, NT_DIMS, preferred_element_type=F32) + yhat[i]
            S[g] = S[g] * pro[g]["g_end"][c * C:c * C + 1, :] + _bdot(Sb, m2t[i]) + dst[i]
    for g in range(ng):
        s_ref[g] = S[g]

    for g in range(ng):
        ln = slice(g * L, (g + 1) * L)
        y = jnp.concatenate(ys[g], axis=0)
        mean = headsum(y) * (1.0 / HEAD)
        yc = y - mean
        var = headsum(yc * yc) * (1.0 / HEAD)
        yn = yc * lax.rsqrt(var + GN_EPS) * lnw_ref[:, ln] + lnb_ref[:, ln]
        yn = yn + headsum(pro[g]["r"] * pro[g]["k"] * rk_ref[:, ln]) * pro[g]["v"]
        o_ref[:, ln] = (yn * g_ref[:, ln]).astype(o_ref.dtype)


def _rwkv_scan(r, k, v, wl, al, g, vmix, params, bsz, seq, tc=256):
    d = D_MODEL
    ng = RWKV_GROUPS_PER_STEP
    W = ng * RWKV_LANES
    to3 = lambda t: t.reshape(bsz, seq, d)
    blk = pl.BlockSpec((None, tc, W), lambda b, h, t: (b, t, h))
    pblk = pl.BlockSpec((1, W), lambda b, h, t: (0, h))
    args = [to3(t) for t in (r, k, v, wl, al, g)]
    in_specs = [blk] * 6
    if vmix is not None:
        args += [to3(vmix[0]), to3(vmix[1]), vmix[2].reshape(1, d)]
        in_specs += [blk, blk, pblk]
    args += [p.reshape(1, d) for p in params]
    in_specs += [pblk] * 7
    out = pl.pallas_call(
        functools.partial(_rwkv_scan_kernel, tc=tc, ng=ng, has_vmix=vmix is not None),
        grid=(bsz, d // W, seq // tc),
        in_specs=in_specs,
        out_specs=blk,
        out_shape=jax.ShapeDtypeStruct((bsz, seq, d), BF16),
        scratch_shapes=[pltpu.VMEM((ng, RWKV_LANES, RWKV_LANES), F32)],
        compiler_params=_cparams(("parallel", "parallel", "arbitrary")),
        name="rwkv_scan",
    )(*args)
    return out.reshape(bsz * seq, d)


def _ssd_kernel(x_ref, b_ref, c_ref, dt_ref, z_ref, dtb_ref, alog_ref, dsk_ref, nw_ref,
                o_ref, ssq_ref, s_ref, *, tc):
    C = CHUNK
    GL = SSM_GL
    NC = tc // C
    g = pl.program_id(2)

    @pl.when((pl.program_id(1) == 0))
    def _():
        s_ref[g] = jnp.zeros((SSM_STATE, GL), F32)

    ej = lax.broadcasted_iota(jnp.int32, (128, GL), 0)
    ec = lax.broadcasted_iota(jnp.int32, (128, GL), 1) // HEAD
    expand = jnp.where(ej == g * SSM_HPG + ec, 1.0, 0.0).astype(BF16)
    _, chunk_tri = _chunk_masks(tc)
    ones_cc = jnp.ones((C, C), BF16)
    t_i = lax.broadcasted_iota(jnp.int32, (C, GL), 0)
    s_i = lax.broadcasted_iota(jnp.int32, (C, GL), 1) % HEAD
    incl = s_i <= t_i
    eye_cat = jnp.where(s_i == t_i, 1.0, 0.0).astype(F32)
    ri = lax.broadcasted_iota(jnp.int32, (GL, GL), 0) // HEAD
    ci = lax.broadcasted_iota(jnp.int32, (GL, GL), 1) // HEAD
    bd_b = jnp.where(ri == ci, 1.0, 0.0).astype(BF16)

    dt = _softplus(dt_ref[...] + dtb_ref[...])
    a_neg = -jnp.exp(alog_ref[...])
    acs_n = _sel_dot_l(chunk_tri, dt * a_neg)
    dt_x = _sel_dot_r(dt, expand)
    acs = _sel_dot_r(acs_n, expand)
    dsk_x = _sel_dot_r(jnp.broadcast_to(dsk_ref[...], (8, 128)), expand)[0:1, :]

    x = x_ref[...].astype(F32)
    xdt = x * dt_x
    xdt_b = xdt.astype(BF16)
    e_acs = jnp.exp(acs)
    b_b = b_ref[...].astype(BF16)
    c_b = c_ref[...].astype(BF16)

    cs = range(NC)
    sl = [slice(c * C, (c + 1) * C) for c in cs]
    acs_end = [acs[c * C + C - 1:c * C + C, :] for c in cs]
    zrow = [_sel_dot_l(ones_cc, acs[sl[c]] * eye_cat) for c in cs]
    cb = [lax.dot_general(c_b[sl[c]], jnp.concatenate([b_b[sl[c]]] * SSM_HPG, axis=0), NT_DIMS,
                          preferred_element_type=F32) for c in cs]
    upd = [lax.dot_general(b_b[sl[c]], (xdt[sl[c]] * jnp.exp(acs_end[c] - acs[sl[c]])).astype(BF16),
                           TN_DIMS, preferred_element_type=F32) for c in cs]
    m = [(cb[c] * jnp.where(incl, jnp.exp(jnp.minimum(acs[sl[c]] - zrow[c], 0.0)), 0.0)).astype(BF16)
         for c in cs]
    y_in = [_bdot(m[c], jnp.concatenate([xdt_b[sl[c]]] * SSM_HPG, axis=0) * bd_b) for c in cs]
    S = s_ref[g]
    S_in = []
    for c in cs:
        S_in.append(S.astype(BF16))
        S = S * jnp.exp(acs_end[c]) + upd[c]
    s_ref[g] = S
    y_st = [_bdot(c_b[sl[c]], S_in[c]) for c in cs]
    y = jnp.concatenate([y_in[c] + y_st[c] * e_acs[sl[c]] for c in cs], axis=0) + x * dsk_x

    yz = y * _silu(z_ref[...].astype(F32))
    part = jnp.sum(yz * yz, axis=-1, keepdims=True)

    @pl.when(g == 0)
    def _():
        ssq_ref[...] = jnp.zeros_like(ssq_ref)

    ssq_ref[...] += jnp.broadcast_to(part, ssq_ref.shape)
    o_ref[...] = (yz * nw_ref[...]).astype(o_ref.dtype)


def _ssd_scan(xbc, dt, z, dt_bias, a_log, d_skip, norm_w, bsz, seq, tc=512):
    G = SSM_GROUPS
    GL = SSM_GL
    xbc3 = xbc.reshape(bsz, seq, SSM_CONV_DIM)
    dt3 = dt.reshape(bsz, seq, 128)
    z3 = z.reshape(bsz, seq, SSM_INNER)
    pad128 = lambda t: jnp.pad(t.reshape(1, SSM_HEADS), ((0, 0), (0, 128 - SSM_HEADS)))
    x_spec = pl.BlockSpec((None, tc, GL), lambda b, t, g: (b, t, g))
    b_spec = pl.BlockSpec((None, tc, SSM_STATE), lambda b, t, g: (b, t, SSM_INNER // SSM_STATE + g))
    c_spec = pl.BlockSpec((None, tc, SSM_STATE), lambda b, t, g: (b, t, (SSM_INNER + SSM_BC) // SSM_STATE + g))
    dt_spec = pl.BlockSpec((None, tc, 128), lambda b, t, g: (b, t, 0))
    p128 = pl.BlockSpec((1, 128), lambda b, t, g: (0, 0))
    out, ssq = pl.pallas_call(
        functools.partial(_ssd_kernel, tc=tc),
        grid=(bsz, seq // tc, G),
        in_specs=[x_spec, b_spec, c_spec, dt_spec, x_spec, p128, p128, p128,
                  pl.BlockSpec((1, GL), lambda b, t, g: (0, g))],
        out_specs=[x_spec, dt_spec],
        out_shape=[jax.ShapeDtypeStruct((bsz, seq, SSM_INNER), BF16),
                   jax.ShapeDtypeStruct((bsz, seq, 128), F32)],
        scratch_shapes=[pltpu.VMEM((G, SSM_STATE, GL), F32)],
        compiler_params=_cparams(("parallel", "arbitrary", "arbitrary")),
        name="ssd_scan",
    )(xbc3, xbc3, xbc3, dt3, z3, pad128(dt_bias), pad128(a_log), pad128(d_skip),
      norm_w.reshape(1, SSM_INNER))
    return out.reshape(bsz * seq, SSM_INNER), ssq.reshape(bsz * seq, 128)


def _pad_cols(w, n):
    return jnp.pad(w, ((0, 0), (0, n - w.shape[1])))


def _pad_rows(w, n):
    return jnp.pad(w, ((0, n - w.shape[0]), (0, 0)))


def kernel(x, c, ada_w, ada_b, ada_table, norm_mix_pre, norm_mix_post, norm_ffn_pre, norm_ffn_post, rwkv_mu, rwkv_w_rkv, rwkv_w0, rwkv_w1, rwkv_w2, rwkv_a0, rwkv_a1, rwkv_a2, rwkv_v0, rwkv_v1, rwkv_v2, rwkv_g1, rwkv_g2, rwkv_k_k, rwkv_k_a, rwkv_r_k, rwkv_ln_w, rwkv_ln_b, rwkv_w_o, ssm_w_in, ssm_conv_w, ssm_conv_b, ssm_dt_bias, ssm_a_log, ssm_d, ssm_norm, ssm_w_out, ffn_w_in, ffn_conv_w, ffn_conv_b, ffn_w_out):
    bsz, seq, d = x.shape
    depth = ada_table.shape[0]
    n = bsz * seq
    bf = lambda t: t.astype(BF16)

    sc_in = jnp.pad(bf(jax.nn.silu(c)), ((0, 16 - bsz), (0, 0)))
    mod = _mm(sc_in, bf(ada_w), bias=ada_b)[:bsz]

    xf = x.reshape(n, d)
    v_first = None
    for layer in range(depth):
        sh_m, sc_m, g_m, sh_f, sc_f, g_f = jnp.split(mod + ada_table[layer], N_MOD, axis=-1)
        idx = layer // 2
        if layer % 2 == 0:
            mu = rwkv_mu[idx]
            xr, xk, xv, xw, xa, xg = _prenorm(xf, norm_mix_pre[layer], sc_m, sh_m, mu, seq)
            r = _mm(xr, bf(rwkv_w_rkv[idx, 0]))
            k = _mm(xk, bf(rwkv_w_rkv[idx, 1]))
            v = _mm(xv, bf(rwkv_w_rkv[idx, 2]))
            wl = _mm(_mm(xw, bf(_pad_cols(rwkv_w1[idx], 128)), act="tanh", out_dtype=BF16),
                     bf(_pad_rows(rwkv_w2[idx], 128)))
            al = _mm(_mm(xa, bf(_pad_cols(rwkv_a1[idx], 128)), out_dtype=BF16),
                     bf(_pad_rows(rwkv_a2[idx], 128)))
            gg = _mm(_mm(xg, bf(rwkv_g1[idx]), act="sigmoid", out_dtype=BF16), bf(rwkv_g2[idx]))
            if idx == 0:
                vmix = None
                v_first = v
            else:
                vl = _mm(_mm(xv, bf(_pad_cols(rwkv_v1[idx - 1], 128)), out_dtype=BF16),
                         bf(_pad_rows(rwkv_v2[idx - 1], 128)))
                vmix = (v_first, vl, rwkv_v0[idx - 1])
            params = (rwkv_w0[idx], rwkv_a0[idx], rwkv_k_k[idx], rwkv_k_a[idx],
                      rwkv_r_k[idx].reshape(d), rwkv_ln_w[idx], rwkv_ln_b[idx])
            yg = _rwkv_scan(r, k, v, wl, al, gg, vmix, params, bsz, seq)
            xf = _mm_post(yg, bf(rwkv_w_o[idx]), xf, g_m, norm_mix_post[layer], seq)
        else:
            (h,) = _prenorm(xf, norm_mix_pre[layer], sc_m, sh_m, None, seq)
            w_in = ssm_w_in[idx]
            z = _mm(h, bf(w_in[:, :SSM_INNER]), out_dtype=BF16)
            xbc = _mm_conv(h, [bf(w_in[:, SSM_INNER:SSM_INNER + SSM_CONV_DIM])], [ssm_conv_w[idx]],
                           [ssm_conv_b[idx]], seq, out_dtype=BF16, nh=2)
            dt = _mm(h, bf(_pad_cols(w_in[:, SSM_INNER + SSM_CONV_DIM:], 128)))
            yz, ssq = _ssd_scan(xbc, dt, z, ssm_dt_bias[idx], ssm_a_log[idx], ssm_d[idx], ssm_norm[idx],
                                bsz, seq)
            xf = _mm_post(yz, bf(ssm_w_out[idx]), xf, g_m, norm_mix_post[layer], seq,
                          ssq=ssq, ssq_dim=SSM_INNER)
        (h,) = _prenorm(xf, norm_ffn_pre[layer], sc_f, sh_f, None, seq)
        w_in = ffn_w_in[layer]
        cw = ffn_conv_w[layer]
        cb = ffn_conv_b[layer]
        F = FFN_HIDDEN
        act = _mm_conv(h,
                       [bf(_pad_cols(w_in[:, :F], FFN_PAD)), bf(_pad_cols(w_in[:, F:], FFN_PAD))],
                       [_pad_cols(cw[:, :F], FFN_PAD), _pad_cols(cw[:, F:], FFN_PAD)],
                       [jnp.pad(cb[:F], (0, FFN_PAD - F)), jnp.pad(cb[F:], (0, FFN_PAD - F))],
                       seq, out_dtype=BF16)
        xf = _mm_post(act, bf(_pad_rows(ffn_w_out[layer], FFN_PAD)), xf, g_f, norm_ffn_post[layer], seq)
    return xf.reshape(bsz, seq, d)
```
